```python
import math
import jax, jax.numpy as jnp
from jax import lax
import numpy as np

D_MODEL = 1024
BATCH = 8
SEQ = 2048
DEPTH = 2

N_A_LAYERS = DEPTH // 2
N_B_LAYERS = DEPTH - N_A_LAYERS
EPS = 1e-5

SSM_EXPAND = 2
SSM_INNER = SSM_EXPAND * D_MODEL
SSM_HEAD_DIM = 64
SSM_HEADS = SSM_INNER // SSM_HEAD_DIM
SSM_GROUPS = 8
SSM_STATE = 128
SSM_CONV = 4
SSM_CHUNK = 256
SSM_CONV_DIM = SSM_INNER + 2 * SSM_GROUPS * SSM_STATE
SSM_PROJ = 2 * SSM_INNER + 2 * SSM_GROUPS * SSM_STATE + SSM_HEADS

ATT_HEAD_DIM = 64
ATT_Q_HEADS = D_MODEL // ATT_HEAD_DIM
ATT_KV_HEADS = 4
ATT_GROUP = ATT_Q_HEADS // ATT_KV_HEADS
WINDOW = 128
ROPE_THETA = 10000.0

FFN_DIM = 2816
FFN_CONV = 3

kernel_name = "yoco_mamba2_swa_sink_convffn"


def rms_norm(x, g):
    xf = x.astype(jnp.float32)
    xf = xf * lax.rsqrt(jnp.mean(xf * xf, axis=-1, keepdims=True) + EPS)
    return xf.astype(x.dtype) * g


def group_rms_norm(y, g, groups):
    b, s, c = y.shape
    yf = y.astype(jnp.float32).reshape(b, s, groups, c // groups)
    yf = yf * lax.rsqrt(jnp.mean(yf * yf, axis=-1, keepdims=True) + EPS)
    return yf.reshape(b, s, c).astype(y.dtype) * g


def causal_dwconv(x, w, bias):
    width, ch = w.shape
    out = lax.conv_general_dilated(
        x, w[:, None, :].astype(x.dtype), window_strides=(1,),
        padding=[(width - 1, 0)], dimension_numbers=("NWC", "WIO", "NWC"),
        feature_group_count=ch)
    return out + bias


def rotary(x, positions):
    half = x.shape[-1] // 2
    inv_freq = ROPE_THETA ** (-jnp.arange(half, dtype=jnp.float32) / half)
    ang = positions.astype(jnp.float32)[..., None] * inv_freq
    cos = jnp.cos(ang)[:, :, None, :]
    sin = jnp.sin(ang)[:, :, None, :]
    xf = x.astype(jnp.float32)
    x1, x2 = xf[..., :half], xf[..., half:]
    return jnp.concatenate([x1 * cos - x2 * sin, x2 * cos + x1 * sin], axis=-1).astype(x.dtype)


def ssd_chunked(x, dt, A, Bm, Cm):
    b, s, h, p = x.shape
    g, n = Bm.shape[2], Bm.shape[3]
    e = h // g
    pad = (-s) % SSM_CHUNK
    x = jnp.pad(x, ((0, 0), (0, pad), (0, 0), (0, 0)))
    dt = jnp.pad(dt, ((0, 0), (0, pad), (0, 0)))
    Bm = jnp.pad(Bm, ((0, 0), (0, pad), (0, 0), (0, 0)))
    Cm = jnp.pad(Cm, ((0, 0), (0, pad), (0, 0), (0, 0)))
    L = SSM_CHUNK
    c = (s + pad) // L
    xd = (x * dt[..., None]).reshape(b, c, L, g, e, p)
    a = (dt * A).reshape(b, c, L, g, e)
    a_cum = jnp.cumsum(a, axis=2)
    Bc = Bm.reshape(b, c, L, g, n)
    Cc = Cm.reshape(b, c, L, g, n)
    seg = a_cum[:, :, :, None] - a_cum[:, :, None, :]
    causal = jnp.tril(jnp.ones((L, L), dtype=bool))[None, None, :, :, None, None]
    decay = jnp.exp(jnp.where(causal, seg, -jnp.inf))
    cb = jnp.einsum("bclgn,bcsgn->bclsg", Cc, Bc)
    w = cb[..., None] * decay
    y_diag = jnp.einsum("bclsge,bcsgep->bclgep", w, xd)
    decay_to_end = jnp.exp(a_cum[:, :, -1:] - a_cum)
    states = jnp.einsum("bclgn,bclge,bclgep->bcgepn", Bc, decay_to_end, xd)
    chunk_decay = jnp.exp(a_cum[:, :, -1])

    def step(state, inp):
        dec, new = inp
        return state * dec[..., None, None] + new, state

    init = jnp.zeros((b, g, e, p, n), jnp.float32)
    _, prev = lax.scan(step, init, (jnp.moveaxis(chunk_decay, 1, 0), jnp.moveaxis(states, 1, 0)))
    prev = jnp.moveaxis(prev, 0, 1)
    y_off = jnp.einsum("bclgn,bcgepn->bclgep", Cc, prev) * jnp.exp(a_cum)[..., None]
    y = (y_diag + y_off).reshape(b, c * L, h, p)
    return y[:, :s]


def mamba2_mixer(h, in_proj, conv_w, conv_b, dt_bias, A_log, D, gnorm, out_proj):
    b, s, _ = h.shape
    zxbcdt = h @ in_proj
    z, xBC, dt = jnp.split(zxbcdt, [SSM_INNER, SSM_INNER + SSM_CONV_DIM], axis=-1)
    xBC = jax.nn.silu(causal_dwconv(xBC, conv_w, conv_b))
    xs, Bm, Cm = jnp.split(xBC, [SSM_INNER, SSM_INNER + SSM_GROUPS * SSM_STATE], axis=-1)
    xs = xs.reshape(b, s, SSM_HEADS, SSM_HEAD_DIM).astype(jnp.float32)
    Bm = Bm.reshape(b, s, SSM_GROUPS, SSM_STATE).astype(jnp.float32)
    Cm = Cm.reshape(b, s, SSM_GROUPS, SSM_STATE).astype(jnp.float32)
    dt = jax.nn.softplus(dt.astype(jnp.float32) + dt_bias.astype(jnp.float32))
    A = -jnp.exp(A_log.astype(jnp.float32))
    y = ssd_chunked(xs, dt, A, Bm, Cm) + xs * D.astype(jnp.float32)[:, None]
    y = y.reshape(b, s, SSM_INNER).astype(h.dtype)
    y = group_rms_norm(y * jax.nn.silu(z), gnorm, SSM_GROUPS)
    return y @ out_proj


def sliding_window_sink_attention(q, k, v, sinks):
    b, s, _, d = q.shape
    nb = s // WINDOW
    qb = q.reshape(b, nb, WINDOW, ATT_KV_HEADS, ATT_GROUP, d)
    kb = k.reshape(b, nb, WINDOW, ATT_KV_HEADS, d)
    vb = v.reshape(b, nb, WINDOW, ATT_KV_HEADS, d)
    blk_pad = ((0, 0), (1, 0), (0, 0), (0, 0), (0, 0))
    k_band = jnp.concatenate([jnp.pad(kb, blk_pad)[:, :-1], kb], axis=2)
    v_band = jnp.concatenate([jnp.pad(vb, blk_pad)[:, :-1], vb], axis=2)
    scores = jnp.einsum("bnqhgd,bnkhd->bnhgqk", qb, k_band).astype(jnp.float32) * (d ** -0.5)
    qi = jnp.arange(WINDOW)[:, None]
    ki = jnp.arange(2 * WINDOW)[None, :]
    rel = qi + WINDOW - ki
    kpos = jnp.arange(nb)[:, None, None] * WINDOW + ki[None] - WINDOW
    mask = (rel >= 0)[None] & (rel < WINDOW)[None] & (kpos >= 0)
    scores = jnp.where(mask[None, :, None, None], scores, -jnp.inf)
    sink = jnp.broadcast_to(
        sinks.astype(jnp.float32).reshape(ATT_KV_HEADS, ATT_GROUP)[None, None, :, :, None, None],
        scores.shape[:-1] + (1,))
    probs = jax.nn.softmax(jnp.concatenate([scores, sink], axis=-1), axis=-1)[..., :-1]
    out = jnp.einsum("bnhgqk,bnkhd->bnqhgd", probs.astype(v.dtype), v_band)
    return out.reshape(b, s, ATT_Q_HEADS * d)


def conv_ffn(x, norm_g, w_in, conv_w, conv_b, w_down):
    h = rms_norm(x, norm_g)
    gate, val = jnp.split(h @ w_in, [FFN_DIM], axis=-1)
    gate = causal_dwconv(gate, conv_w, conv_b)
    return (jax.nn.silu(gate) * val) @ w_down


def _dense(key, shape, fan_in):
    return jax.random.normal(key, shape, jnp.float32) * (fan_in ** -0.5)


def _gain(key, shape):
    return 1.0 + 0.02 * jax.random.normal(key, shape, jnp.float32)


def _small(key, shape):
    return 0.02 * jax.random.normal(key, shape, jnp.float32)


def setup_inputs(seed: int = 0) -> dict:
    key = jax.random.key(seed)
    ks = jax.random.split(key, 32)
    qkv_dim = ATT_KV_HEADS * ATT_HEAD_DIM
    q_dim = ATT_Q_HEADS * ATT_HEAD_DIM
    x = jax.random.normal(ks[0], (BATCH, SEQ, D_MODEL), jnp.float32)
    positions = (jnp.arange(SEQ, dtype=jnp.int32)[None, :]
                 + jax.random.randint(ks[1], (BATCH, 1), 0, 4096, dtype=jnp.int32))
    dt0 = jnp.exp(jax.random.uniform(ks[6], (N_A_LAYERS, SSM_HEADS), jnp.float32,
                                     minval=math.log(1e-3), maxval=math.log(1e-1)))
    return {
        "x": x,
        "positions": positions,
        "a_norm": _gain(ks[2], (N_A_LAYERS, D_MODEL)),
        "a_in_proj": _dense(ks[3], (N_A_LAYERS, D_MODEL, SSM_PROJ), D_MODEL),
        "a_conv_w": jax.random.normal(ks[4], (N_A_LAYERS, SSM_CONV, SSM_CONV_DIM), jnp.float32) * (SSM_CONV ** -0.5),
        "a_conv_b": _small(ks[5], (N_A_LAYERS, SSM_CONV_DIM)),
        "a_dt_bias": dt0 + jnp.log(-jnp.expm1(-dt0)),
        "a_A_log": jnp.log(jax.random.uniform(ks[7], (N_A_LAYERS, SSM_HEADS), jnp.float32, minval=1.0, maxval=16.0)),
        "a_D": 1.0 + 0.1 * jax.random.normal(ks[8], (N_A_LAYERS, SSM_HEADS), jnp.float32),
        "a_gnorm": _gain(ks[9], (N_A_LAYERS, SSM_INNER)),
        "a_out_proj": _dense(ks[10], (N_A_LAYERS, SSM_INNER, D_MODEL), SSM_INNER),
        "kv_norm": _gain(ks[11], (D_MODEL,)),
        "w_kv": _dense(ks[12], (D_MODEL, 2 * qkv_dim), D_MODEL),
        "b_kv": _small(ks[13], (2 * qkv_dim,)),
        "k_norm": _gain(ks[14], (ATT_HEAD_DIM,)),
        "b_norm": _gain(ks[15], (N_B_LAYERS, D_MODEL)),
        "w_q": _dense(ks[16], (N_B_LAYERS, D_MODEL, q_dim), D_MODEL),
        "b_q": _small(ks[17], (N_B_LAYERS, q_dim)),
        "q_norm": _gain(ks[18], (N_B_LAYERS, ATT_HEAD_DIM)),
        "sinks": jax.random.normal(ks[19], (N_B_LAYERS, ATT_Q_HEADS), jnp.float32),
        "w_o": _dense(ks[20], (N_B_LAYERS, q_dim, D_MODEL), q_dim),
        "b_o": _small(ks[21], (N_B_LAYERS, D_MODEL)),
        "f_norm": _gain(ks[22], (DEPTH, D_MODEL)),
        "f_w_in": _dense(ks[23], (DEPTH, D_MODEL, 2 * FFN_DIM), D_MODEL),
        "f_conv_w": jax.random.normal(ks[24], (DEPTH, FFN_CONV, FFN_DIM), jnp.float32) * (FFN_CONV ** -0.5),
        "f_conv_b": _small(ks[25], (DEPTH, FFN_DIM)),
        "f_w_down": _dense(ks[26], (DEPTH, FFN_DIM, D_MODEL), FFN_DIM),
    }


def reference(x, positions, a_norm, a_in_proj, a_conv_w, a_conv_b, a_dt_bias, a_A_log, a_D,
              a_gnorm, a_out_proj, kv_norm, w_kv, b_kv, k_norm, b_norm, w_q, b_q, q_norm,
              sinks, w_o, b_o, f_norm, f_w_in, f_conv_w, f_conv_b, f_w_down):
    b, s, _ = x.shape
    k_shared = None
    v_shared = None
    for layer in range(DEPTH):
        if layer < N_A_LAYERS:
            i = layer
            h = rms_norm(x, a_norm[i])
            x = x + mamba2_mixer(h, a_in_proj[i], a_conv_w[i], a_conv_b[i], a_dt_bias[i],
                                 a_A_log[i], a_D[i], a_gnorm[i], a_out_proj[i])
        else:
            i = layer - N_A_LAYERS
            if i == 0:
                kv = rms_norm(x, kv_norm) @ w_kv + b_kv
                k_shared, v_shared = jnp.split(kv, 2, axis=-1)
                k_shared = k_shared.reshape(b, s, ATT_KV_HEADS, ATT_HEAD_DIM)
                v_shared = v_shared.reshape(b, s, ATT_KV_HEADS, ATT_HEAD_DIM)
                k_shared = rotary(rms_norm(k_shared, k_norm), positions)
            h = rms_norm(x, b_norm[i])
            q = (h @ w_q[i] + b_q[i]).reshape(b, s, ATT_Q_HEADS, ATT_HEAD_DIM)
            q = rotary(rms_norm(q, q_norm[i]), positions)
            att = sliding_window_sink_attention(q, k_shared, v_shared, sinks[i])
            x = x + att @ w_o[i] + b_o[i]
        x = x + conv_ffn(x, f_norm[layer], f_w_in[layer], f_conv_w[layer], f_conv_b[layer], f_w_down[layer])
    return x
```

```python
import functools

import jax
import jax.numpy as jnp
from jax import lax
from jax.experimental import pallas as pl
from jax.experimental.pallas import tpu as pltpu

F32 = jnp.float32
BF16 = jnp.bfloat16

D_MODEL = 1024
EPS = 1e-5

SSM_INNER = 2048
SSM_HEAD_DIM = 64
SSM_HEADS = 32
SSM_GROUPS = 8
SSM_STATE = 128
SSM_CONV = 4
SSM_CHUNK = 256
SSM_XBC = SSM_INNER + 2 * SSM_GROUPS * SSM_STATE
SSM_GROUP_WIDTH = SSM_INNER // SSM_GROUPS

ATT_HEAD_DIM = 64
ATT_Q_HEADS = 16
ATT_KV_HEADS = 4
ATT_GROUP = 4
WINDOW = 128
ROPE_THETA = 10000.0

FFN_DIM = 2816
FFN_CONV = 3

LANES = 128
CARRY_ROWS = 8
VMEM_LIMIT = 56 * 1024 * 1024


def _params(sem):
    return pltpu.CompilerParams(dimension_semantics=sem, vmem_limit_bytes=VMEM_LIMIT)


def _const_spec(shape):
    nd = len(shape)
    return pl.BlockSpec(shape, lambda *_: (0,) * nd, pipeline_mode=pl.Buffered(1))


def _rms_scale(x):
    return lax.rsqrt(jnp.mean(x * x, axis=-1, keepdims=True) + EPS)


def _silu(x):
    return x * jax.nn.sigmoid(x)


def _causal_conv(ext_ref, w_ref, b_ref, c0, width, tm, taps):
    base = CARRY_ROWS - (taps - 1)
    acc = b_ref[:, c0:c0 + width] + w_ref[0:1, c0:c0 + width] * ext_ref[base:base + tm, c0:c0 + width]
    for k in range(1, taps):
        acc = acc + w_ref[k:k + 1, c0:c0 + width] * ext_ref[base + k:base + k + tm, c0:c0 + width]
    return acc


def _mamba_in_kernel(x_ref, g_ref, w_ref, wdt_ref, cw_ref, cb_ref, dtb_ref,
                     z_ref, xbc_ref, dt_ref, ext_ref, *, tm, chunk):
    @pl.when(pl.program_id(1) == 0)
    def _():
        ext_ref[0:CARRY_ROWS, :] = jnp.zeros((CARRY_ROWS, SSM_XBC), F32)

    x = x_ref[...]
    h = (x * _rms_scale(x) * g_ref[...]).astype(BF16)
    for c in range(0, SSM_INNER, chunk):
        z_ref[:, c:c + chunk] = jnp.dot(h, w_ref[:, c:c + chunk], preferred_element_type=F32).astype(BF16)
    for c in range(0, SSM_XBC, chunk):
        pre = jnp.dot(h, w_ref[:, SSM_INNER + c:SSM_INNER + c + chunk], preferred_element_type=F32)
        ext_ref[CARRY_ROWS:CARRY_ROWS + tm, c:c + chunk] = pre
        conv = _causal_conv(ext_ref, cw_ref, cb_ref, c, chunk, tm, SSM_CONV)
        xbc_ref[:, c:c + chunk] = _silu(conv).astype(BF16)
        ext_ref[0:CARRY_ROWS, c:c + chunk] = ext_ref[tm:tm + CARRY_ROWS, c:c + chunk]
    dtp = jnp.dot(h, wdt_ref[...], preferred_element_type=F32) + dtb_ref[...]
    dt_ref[...] = jnp.maximum(dtp, 0.0) + jnp.log1p(jnp.exp(-jnp.abs(dtp)))


def _mamba_in(x2, g, w, wdt, cw, cb, dtb, *, batch, seq, tm=512, chunk=512):
    nst = seq // tm
    row = lambda b, i: (b * nst + i, 0)
    t = batch * seq
    return pl.pallas_call(
        functools.partial(_mamba_in_kernel, tm=tm, chunk=chunk),
        grid=(batch, nst),
        in_specs=[
            pl.BlockSpec((tm, D_MODEL), row),
            _const_spec((1, D_MODEL)),
            _const_spec((D_MODEL, SSM_INNER + SSM_XBC)),
            _const_spec((D_MODEL, LANES)),
            _const_spec((SSM_CONV, SSM_XBC)),
            _const_spec((1, SSM_XBC)),
            _const_spec((1, LANES)),
        ],
        out_specs=[
            pl.BlockSpec((tm, SSM_INNER), row),
            pl.BlockSpec((tm, SSM_XBC), row),
            pl.BlockSpec((tm, LANES), row),
        ],
        out_shape=[
            jax.ShapeDtypeStruct((t, SSM_INNER), BF16),
            jax.ShapeDtypeStruct((t, SSM_XBC), BF16),
            jax.ShapeDtypeStruct((t, LANES), F32),
        ],
        scratch_shapes=[pltpu.VMEM((tm + CARRY_ROWS, SSM_XBC), F32)],
        compiler_params=_params(("arbitrary", "arbitrary")),
        name="mamba_in",
    )(x2, g, w, wdt, cw, cb, dtb)


def _lane_expand(cols, width):
    n = len(cols)
    L = cols[0].shape[0]
    grp = lax.broadcasted_iota(jnp.int32, (L, n * width), 1) // width
    out = jnp.broadcast_to(cols[n - 1], (L, n * width))
    for e in range(n - 2, -1, -1):
        out = jnp.where(grp == e, jnp.broadcast_to(cols[e], (L, n * width)), out)
    return out


def _ssd_kernel(xbc_ref, z_ref, dt_ref, x_ref, alog_ref, dx_ref, gn_ref, wo_ref,
                o_ref, state_ref, yn_ref):
    L = SSM_CHUNK
    GW = SSM_GROUP_WIDTH
    HPG = SSM_HEADS // SSM_GROUPS

    @pl.when(pl.program_id(1) == 0)
    def _():
        state_ref[...] = jnp.zeros_like(state_ref)

    dt = dt_ref[...]
    a = dt * (-jnp.exp(alog_ref[...]))
    row = lax.broadcasted_iota(jnp.int32, (L, L), 0)
    col = lax.broadcasted_iota(jnp.int32, (L, L), 1)
    causal = row >= col
    tri = jnp.where(causal, 1.0, 0.0).astype(BF16)
    a_hi = a.astype(BF16)
    r1 = a - a_hi.astype(F32)
    a_mid = r1.astype(BF16)
    a_lo = (r1 - a_mid.astype(F32)).astype(BF16)
    acum = (jnp.dot(tri, a_hi, preferred_element_type=F32)
            + jnp.dot(tri, a_mid, preferred_element_type=F32)
            + jnp.dot(tri, a_lo, preferred_element_type=F32))
    acum_t = jnp.transpose(acum)
    a_last = acum[L - 1:L, :]
    exp_a = jnp.exp(acum)
    dte = jnp.exp(a_last - acum)
    cdec = jnp.exp(a_last)

    lane_grp = lax.broadcasted_iota(jnp.int32, (L, GW), 1) // SSM_HEAD_DIM

    for g in range(SSM_GROUPS):
        heads = [g * HPG + e for e in range(HPG)]
        xs = xbc_ref[:, g * GW:(g + 1) * GW].astype(F32)
        bg = xbc_ref[:, SSM_INNER + g * SSM_STATE:SSM_INNER + (g + 1) * SSM_STATE]
        cg = xbc_ref[:, SSM_INNER + SSM_GROUPS * SSM_STATE + g * SSM_STATE:
                     SSM_INNER + SSM_GROUPS * SSM_STATE + (g + 1) * SSM_STATE]
        bgt = jnp.transpose(bg.astype(F32)).astype(BF16)
        cb = jnp.dot(cg, bgt, preferred_element_type=F32)

        dtx = _lane_expand([dt[:, h:h + 1] for h in heads], SSM_HEAD_DIM)
        xd = xs * dtx
        xd_b = xd.astype(BF16)

        w_parts, x_parts = [], []
        for e, h in enumerate(heads):
            seg = acum[:, h:h + 1] - acum_t[h:h + 1, :]
            decay = jnp.exp(jnp.where(causal, seg, -jnp.inf))
            w_parts.append((cb * decay).astype(BF16))
            x_parts.append(jnp.where(lane_grp == e, xd_b, jnp.zeros_like(xd_b)))
        y = jnp.dot(jnp.concatenate(w_parts, axis=1), jnp.concatenate(x_parts, axis=0),
                    preferred_element_type=F32)

        st = state_ref[g]
        ea = _lane_expand([exp_a[:, h:h + 1] for h in heads], SSM_HEAD_DIM)
        y = y + jnp.dot(cg, st.astype(BF16), preferred_element_type=F32) * ea

        de = _lane_expand([dte[:, h:h + 1] for h in heads], SSM_HEAD_DIM)
        cd = _lane_expand([cdec[:, h:h + 1] for h in heads], SSM_HEAD_DIM)
        state_ref[g] = st * cd + jnp.dot(bgt, (xd * de).astype(BF16), preferred_element_type=F32)

        y = y + xs * dx_ref[:, g * GW:(g + 1) * GW]
        y = y * _silu(z_ref[:, g * GW:(g + 1) * GW].astype(F32))
        yn = y * _rms_scale(y) * gn_ref[:, g * GW:(g + 1) * GW]
        yn_ref[:, g * GW:(g + 1) * GW] = yn.astype(BF16)

    o_ref[...] = x_ref[...] + jnp.dot(yn_ref[...], wo_ref[...], preferred_element_type=F32)


def _ssd(xbc, z, dt, x2, alog, dx, gn, wo, *, batch, seq):
    L = SSM_CHUNK
    nc = seq // L
    row = lambda b, i: (b * nc + i, 0)
    t = batch * seq
    return pl.pallas_call(
        _ssd_kernel,
        grid=(batch, nc),
        in_specs=[
            pl.BlockSpec((L, SSM_XBC), row),
            pl.BlockSpec((L, SSM_INNER), row),
            pl.BlockSpec((L, LANES), row),
            pl.BlockSpec((L, D_MODEL), row),
            _const_spec((1, LANES)),
            _const_spec((1, SSM_INNER)),
            _const_spec((1, SSM_INNER)),
            _const_spec((SSM_INNER, D_MODEL)),
        ],
        out_specs=pl.BlockSpec((L, D_MODEL), row),
        out_shape=jax.ShapeDtypeStruct((t, D_MODEL), F32),
        scratch_shapes=[
            pltpu.VMEM((SSM_GROUPS, SSM_STATE, SSM_GROUP_WIDTH), F32),
            pltpu.VMEM((L, SSM_INNER), BF16),
        ],
        compiler_params=_params(("arbitrary", "arbitrary")),
        name="ssd_out",
    )(xbc, z, dt, x2, alog, dx, gn, wo)


_FFN_CHUNKS = ((0, 512), (512, 512), (1024, 512), (1536, 512), (2048, 512), (2560, 256))


def _ffn_kernel(x_ref, g_ref, win_ref, cw_ref, cb_ref, wd_ref, o_ref, ext_ref, hid_ref, *, tm):
    @pl.when(pl.program_id(1) == 0)
    def _():
        ext_ref[0:CARRY_ROWS, :] = jnp.zeros((CARRY_ROWS, FFN_DIM), F32)

    x = x_ref[...]
    h = (x * _rms_scale(x) * g_ref[...]).astype(BF16)
    for c, width in _FFN_CHUNKS:
        gate = jnp.dot(h, win_ref[:, c:c + width], preferred_element_type=F32)
        val = jnp.dot(h, win_ref[:, FFN_DIM + c:FFN_DIM + c + width], preferred_element_type=F32)
        ext_ref[CARRY_ROWS:CARRY_ROWS + tm, c:c + width] = gate
        conv = _causal_conv(ext_ref, cw_ref, cb_ref, c, width, tm, FFN_CONV)
        hid_ref[:, c:c + width] = (_silu(conv) * val).astype(BF16)
        ext_ref[0:CARRY_ROWS, c:c + width] = ext_ref[tm:tm + CARRY_ROWS, c:c + width]
    o_ref[...] = x + jnp.dot(hid_ref[...], wd_ref[...], preferred_element_type=F32)


def _ffn(x2, g, win, cw, cb, wd, *, batch, seq, tm=512):
    nst = seq // tm
    row = lambda b, i: (b * nst + i, 0)
    t = batch * seq
    return pl.pallas_call(
        functools.partial(_ffn_kernel, tm=tm),
        grid=(batch, nst),
        in_specs=[
            pl.BlockSpec((tm, D_MODEL), row),
            _const_spec((1, D_MODEL)),
            _const_spec((D_MODEL, 2 * FFN_DIM)),
            _const_spec((FFN_CONV, FFN_DIM)),
            _const_spec((1, FFN_DIM)),
            _const_spec((FFN_DIM, D_MODEL)),
        ],
        out_specs=pl.BlockSpec((tm, D_MODEL), row),
        out_shape=jax.ShapeDtypeStruct((t, D_MODEL), F32),
        scratch_shapes=[
            pltpu.VMEM((tm + CARRY_ROWS, FFN_DIM), F32),
            pltpu.VMEM((tm, FFN_DIM), BF16),
        ],
        compiler_params=_params(("arbitrary", "arbitrary")),
        name="conv_ffn",
    )(x2, g, win, cw, cb, wd)


def _rope_kernel(pos_ref, invf_ref, cos_ref, sin_ref):
    ang = pos_ref[...].astype(F32) * invf_ref[...]
    cos_ref[...] = jnp.cos(ang)
    sin_ref[...] = jnp.sin(ang)


def _rope_tables(pos_col, invf, *, tm=1024):
    t = pos_col.shape[0]
    return pl.pallas_call(
        _rope_kernel,
        grid=(t // tm,),
        in_specs=[pl.BlockSpec((tm, 1), lambda i: (i, 0)), _const_spec((1, LANES))],
        out_specs=[pl.BlockSpec((tm, LANES), lambda i: (i, 0))] * 2,
        out_shape=[jax.ShapeDtypeStruct((t, LANES), F32)] * 2,
        compiler_params=_params(("arbitrary",)),
        name="rope_tables",
    )(pos_col, invf)


def _head_norm_rope(y, gain_ref, cos, sin_signed, first_half, ones_bd, scale):
    outs = []
    w = y.shape[1]
    for c in range(0, w, 2 * LANES):
        blk = y[:, c:c + 2 * LANES]
        ss = jnp.dot((blk * blk).astype(BF16), ones_bd, preferred_element_type=F32)
        yn = blk * lax.rsqrt(ss * (1.0 / ATT_HEAD_DIM) + EPS) * gain_ref[:, c:c + 2 * LANES]
        for j in range(0, 2 * LANES, LANES):
            v = yn[:, j:j + LANES]
            lo = pltpu.roll(v, ATT_HEAD_DIM // 2, axis=1)
            hi = pltpu.roll(v, LANES - ATT_HEAD_DIM // 2, axis=1)
            rot = jnp.where(first_half, hi, lo)
            outs.append(((v * cos + rot * sin_signed) * scale).astype(BF16))
    return jnp.concatenate(outs, axis=1)


def _qkv_kernel(x_ref, cos_ref, sin_ref, gkv_ref, gq_ref, wkv_ref, bkv_ref, wq_ref, bq_ref,
                kn_ref, qn_ref, q_ref, k_ref, v_ref, *, tm):
    x = x_ref[...]
    xn = x * _rms_scale(x)
    hkv = (xn * gkv_ref[...]).astype(BF16)
    hq = (xn * gq_ref[...]).astype(BF16)
    kw = ATT_KV_HEADS * ATT_HEAD_DIM

    lane = lax.broadcasted_iota(jnp.int32, (tm, LANES), 1)
    first_half = (lane % ATT_HEAD_DIM) < (ATT_HEAD_DIM // 2)
    cos = cos_ref[...]
    sin_signed = jnp.where(first_half, -sin_ref[...], sin_ref[...])
    r = lax.broadcasted_iota(jnp.int32, (2 * LANES, 2 * LANES), 0) // ATT_HEAD_DIM
    c = lax.broadcasted_iota(jnp.int32, (2 * LANES, 2 * LANES), 1) // ATT_HEAD_DIM
    ones_bd = jnp.where(r == c, 1.0, 0.0).astype(BF16)

    kv = jnp.dot(hkv, wkv_ref[...], preferred_element_type=F32) + bkv_ref[...]
    k_ref[...] = _head_norm_rope(kv[:, :kw], kn_ref, cos, sin_signed, first_half, ones_bd, 1.0)
    v_ref[...] = kv[:, kw:].astype(BF16)
    q = jnp.dot(hq, wq_ref[...], preferred_element_type=F32) + bq_ref[...]
    q_ref[...] = _head_norm_rope(q, qn_ref, cos, sin_signed, first_half, ones_bd, ATT_HEAD_DIM ** -0.5)


def _qkv(x2, cos, sin, gkv, gq, wkv, bkv, wq, bq, kn, qn, *, tm=512):
    t = x2.shape[0]
    kw = ATT_KV_HEADS * ATT_HEAD_DIM
    qw = ATT_Q_HEADS * ATT_HEAD_DIM
    row = lambda i: (i, 0)
    return pl.pallas_call(
        functools.partial(_qkv_kernel, tm=tm),
        grid=(t // tm,),
        in_specs=[
            pl.BlockSpec((tm, D_MODEL), row),
            pl.BlockSpec((tm, LANES), row),
            pl.BlockSpec((tm, LANES), row),
            _const_spec((1, D_MODEL)),
            _const_spec((1, D_MODEL)),
            _const_spec((D_MODEL, 2 * kw)),
            _const_spec((1, 2 * kw)),
            _const_spec((D_MODEL, qw)),
            _const_spec((1, qw)),
            _const_spec((1, kw)),
            _const_spec((1, qw)),
        ],
        out_specs=[
            pl.BlockSpec((tm, qw), row),
            pl.BlockSpec((tm, kw), row),
            pl.BlockSpec((tm, kw), row),
        ],
        out_shape=[
            jax.ShapeDtypeStruct((t, qw), BF16),
            jax.ShapeDtypeStruct((t, kw), BF16),
            jax.ShapeDtypeStruct((t, kw), BF16),
        ],
        compiler_params=_params(("arbitrary",)),
        name="qkv_rope",
    )(x2, cos, sin, gkv, gq, wkv, bkv, wq, bq, kn, qn)


def _attn_kernel(sink_ref, q_ref, k_ref, v_ref, x_ref, wo_ref, bo_ref, o_ref, att_ref, *, tq):
    i = pl.program_id(1)
    W = WINDOW
    lane = lax.broadcasted_iota(jnp.int32, (W, LANES), 1)
    low_half = lane < ATT_HEAD_DIM
    qi = lax.broadcasted_iota(jnp.int32, (W, 2 * W), 0)
    ki = lax.broadcasted_iota(jnp.int32, (W, 2 * W), 1)

    for jb in range(tq // W):
        q0 = i * tq + jb * W
        start = jnp.maximum(q0 - W, 0)
        start = pl.multiple_of(start, W)
        rel = qi + (q0 - start) - ki
        mask = (rel >= 0) & (rel < W)
        kb = k_ref[pl.ds(start, 2 * W), :]
        vb = v_ref[pl.ds(start, 2 * W), :]
        for hk in range(ATT_KV_HEADS):
            kh = kb[:, hk * ATT_HEAD_DIM:(hk + 1) * ATT_HEAD_DIM]
            vh = vb[:, hk * ATT_HEAD_DIM:(hk + 1) * ATT_HEAD_DIM]
            for g in range(ATT_GROUP):
                hq = hk * ATT_GROUP + g
                qh = q_ref[jb * W:(jb + 1) * W, hq * ATT_HEAD_DIM:(hq + 1) * ATT_HEAD_DIM]
                s = lax.dot_general(qh, kh, (((1,), (1,)), ((), ())), preferred_element_type=F32)
                s = jnp.where(mask, s, -jnp.inf)
                sink = sink_ref[hq]
                m = jnp.maximum(jnp.max(s, axis=-1, keepdims=True), sink)
                p = jnp.exp(s - m)
                denom = jnp.sum(p, axis=-1, keepdims=True) + jnp.exp(sink - m)
                pn = (p / denom).astype(BF16)
                o = jnp.dot(pn, vh, preferred_element_type=F32)
                att_ref[jb * W:(jb + 1) * W, hq * ATT_HEAD_DIM:(hq + 1) * ATT_HEAD_DIM] = o.astype(BF16)
    del low_half
    o_ref[...] = (x_ref[...] + bo_ref[...]
                  + jnp.dot(att_ref[...], wo_ref[...], preferred_element_type=F32))


def _attn(sinks, q, k, v, x2, wo, bo, *, batch, seq, tq=512):
    nst = seq // tq
    qw = ATT_Q_HEADS * ATT_HEAD_DIM
    kw = ATT_KV_HEADS * ATT_HEAD_DIM
    row = lambda b, i: (b * nst + i, 0)
    per_batch = lambda b, i: (b, 0)
    t = batch * seq
    return pl.pallas_call(
        functools.partial(_attn_kernel, tq=tq),
        grid=(batch, nst),
        in_specs=[
            pl.BlockSpec(memory_space=pltpu.SMEM),
            pl.BlockSpec((tq, qw), row),
            pl.BlockSpec((seq, kw), per_batch),
            pl.BlockSpec((seq, kw), per_batch),
            pl.BlockSpec((tq, D_MODEL), row),
            _const_spec((qw, D_MODEL)),
            _const_spec((1, D_MODEL)),
        ],
        out_specs=pl.BlockSpec((tq, D_MODEL), row),
        out_shape=jax.ShapeDtypeStruct((t, D_MODEL), F32),
        scratch_shapes=[pltpu.VMEM((tq, qw), BF16)],
        compiler_params=_params(("arbitrary", "arbitrary")),
        name="swa_attn",
    )(sinks, q, k, v, x2, wo, bo)


def _row(v, width=None):
    v = v.reshape(1, -1).astype(F32)
    if width is not None and v.shape[1] < width:
        v = jnp.pad(v, ((0, 0), (0, width - v.shape[1])))
    return v


def kernel(x, positions, a_norm, a_in_proj, a_conv_w, a_conv_b, a_dt_bias, a_A_log, a_D, a_gnorm,
           a_out_proj, kv_norm, w_kv, b_kv, k_norm, b_norm, w_q, b_q, q_norm, sinks, w_o, b_o,
           f_norm, f_w_in, f_conv_w, f_conv_b, f_w_down):
    batch, seq, _ = x.shape
    t = batch * seq
    x2 = x.reshape(t, D_MODEL)

    w_in = a_in_proj[0]
    w_main = w_in[:, :SSM_INNER + SSM_XBC].astype(BF16)
    w_dt = jnp.pad(w_in[:, SSM_INNER + SSM_XBC:], ((0, 0), (0, LANES - SSM_HEADS))).astype(BF16)
    z, xbc, dt = _mamba_in(x2, _row(a_norm[0]), w_main, w_dt, a_conv_w[0].astype(F32),
                           _row(a_conv_b[0]), _row(a_dt_bias[0], LANES), batch=batch, seq=seq)
    dx = jnp.repeat(a_D[0].astype(F32), SSM_HEAD_DIM).reshape(1, SSM_INNER)
    x2 = _ssd(xbc, z, dt, x2, _row(a_A_log[0], LANES), dx, _row(a_gnorm[0]),
              a_out_proj[0].astype(BF16), batch=batch, seq=seq)
    x2 = _ffn(x2, _row(f_norm[0]), f_w_in[0].astype(BF16), f_conv_w[0].astype(F32),
              _row(f_conv_b[0]), f_w_down[0].astype(BF16), batch=batch, seq=seq)

    half = ATT_HEAD_DIM // 2
    inv_freq = ROPE_THETA ** (-jnp.arange(half, dtype=F32) / half)
    invf = jnp.tile(inv_freq, LANES // half).reshape(1, LANES)
    cos, sin = _rope_tables(positions.reshape(t, 1), invf)
    q, k, v = _qkv(x2, cos, sin, _row(kv_norm), _row(b_norm[0]), w_kv.astype(BF16), _row(b_kv),
                   w_q[0].astype(BF16), _row(b_q[0]),
                   _row(jnp.tile(k_norm, ATT_KV_HEADS)), _row(jnp.tile(q_norm[0], ATT_Q_HEADS)))
    x2 = _attn(sinks[0].astype(F32), q, k, v, x2, w_o[0].astype(BF16), _row(b_o[0]),
               batch=batch, seq=seq)
    x2 = _ffn(x2, _row(f_norm[1]), f_w_in[1].astype(BF16), f_conv_w[1].astype(F32),
              _row(f_conv_b[1]), f_w_down[1].astype(BF16), batch=batch, seq=seq)
    return x2.reshape(batch, seq, D_MODEL)
```

```python
import functools

import jax
import jax.numpy as jnp
from jax import lax
from jax.experimental import pallas as pl
from jax.experimental.pallas import tpu as pltpu

F32 = jnp.float32
BF16 = jnp.bfloat16

D_MODEL = 1024
EPS = 1e-5

SSM_INNER = 2048
SSM_HEAD_DIM = 64
SSM_HEADS = 32
SSM_GROUPS = 8
SSM_STATE = 128
SSM_CONV = 4
SSD_BLOCK = 128
LOG2E = 1.4426950408889634
SSM_XBC = SSM_INNER + 2 * SSM_GROUPS * SSM_STATE
SSM_GROUP_WIDTH = SSM_INNER // SSM_GROUPS

ATT_HEAD_DIM = 64
ATT_Q_HEADS = 16
ATT_KV_HEADS = 4
ATT_GROUP = 4
WINDOW = 128
ROPE_THETA = 10000.0

FFN_DIM = 2816
FFN_CONV = 3

LANES = 128
CARRY_ROWS = 8
VMEM_LIMIT = 56 * 1024 * 1024


def _params(sem):
    return pltpu.CompilerParams(dimension_semantics=sem, vmem_limit_bytes=VMEM_LIMIT)


def _const_spec(shape):
    nd = len(shape)
    return pl.BlockSpec(shape, lambda *_: (0,) * nd, pipeline_mode=pl.Buffered(1))


def _rms_scale(x):
    return lax.rsqrt(jnp.mean(x * x, axis=-1, keepdims=True) + EPS)


def _silu(x):
    hx = 0.5 * x
    return hx + hx * jnp.tanh(hx)


def _causal_conv(pre, hist_ref, w_ref, b_ref, c0, width, taps):
    tm = pre.shape[0]
    cols = slice(c0, c0 + width)
    ext = jnp.concatenate([hist_ref[:, cols], pre], axis=0)
    hist_ref[:, cols] = pre[tm - CARRY_ROWS:tm, :]
    w = [w_ref[k:k + 1, cols] for k in range(taps)]
    back1 = pltpu.roll(ext, 1, axis=0)
    if taps == 4:
        near = w[3] * ext + w[2] * back1
        far = w[1] * ext + w[0] * back1
        out = near + pltpu.roll(far, 2, axis=0)
    else:
        assert taps == 3
        far = w[1] * ext + w[0] * back1
        out = w[2] * ext + pltpu.roll(far, 1, axis=0)
    return out[CARRY_ROWS:, :] + b_ref[:, cols]


def _mamba_in_kernel(x_ref, g_ref, w_ref, wdt_ref, cw_ref, cb_ref, dtb_ref,
                     z_ref, xbc_ref, dt_ref, hist_ref, *, tm, chunk):
    @pl.when(pl.program_id(1) == 0)
    def _():
        hist_ref[...] = jnp.zeros_like(hist_ref)

    x = x_ref[...]
    h = (x * _rms_scale(x) * g_ref[...]).astype(BF16)
    for c in range(0, SSM_INNER, chunk):
        z_ref[:, c:c + chunk] = jnp.dot(h, w_ref[:, c:c + chunk], preferred_element_type=F32).astype(BF16)
    for c in range(0, SSM_XBC, chunk):
        pre = jnp.dot(h, w_ref[:, SSM_INNER + c:SSM_INNER + c + chunk], preferred_element_type=F32)
        conv = _causal_conv(pre, hist_ref, cw_ref, cb_ref, c, chunk, SSM_CONV)
        xbc_ref[:, c:c + chunk] = _silu(conv).astype(BF16)
    dtp = jnp.dot(h, wdt_ref[...], preferred_element_type=F32) + dtb_ref[...]
    dt_ref[...] = jnp.maximum(dtp, 0.0) + jnp.log1p(jnp.exp(-jnp.abs(dtp)))


def _mamba_in(x2, g, w, wdt, cw, cb, dtb, *, batch, seq, tm=512, chunk=256):
    nst = seq // tm
    row = lambda b, i: (b * nst + i, 0)
    t = batch * seq
    return pl.pallas_call(
        functools.partial(_mamba_in_kernel, tm=tm, chunk=chunk),
        grid=(batch, nst),
        in_specs=[
            pl.BlockSpec((tm, D_MODEL), row),
            _const_spec((1, D_MODEL)),
            _const_spec((D_MODEL, SSM_INNER + SSM_XBC)),
            _const_spec((D_MODEL, LANES)),
            _const_spec((SSM_CONV, SSM_XBC)),
            _const_spec((1, SSM_XBC)),
            _const_spec((1, LANES)),
        ],
        out_specs=[
            pl.BlockSpec((tm, SSM_INNER), row),
            pl.BlockSpec((tm, SSM_XBC), row),
            pl.BlockSpec((tm, LANES), row),
        ],
        out_shape=[
            jax.ShapeDtypeStruct((t, SSM_INNER), BF16),
            jax.ShapeDtypeStruct((t, SSM_XBC), BF16),
            jax.ShapeDtypeStruct((t, LANES), F32),
        ],
        scratch_shapes=[pltpu.VMEM((CARRY_ROWS, SSM_XBC), F32)],
        compiler_params=_params(("arbitrary", "arbitrary")),
        name="mamba_in",
    )(x2, g, w, wdt, cw, cb, dtb)


def _split3(a):
    hi = a.astype(BF16)
    r1 = a - hi.astype(F32)
    mid = r1.astype(BF16)
    lo = (r1 - mid.astype(F32)).astype(BF16)
    return hi, mid, lo


def _pack3(v, valid):
    hi, mid, lo = _split3(jnp.where(valid, v, 0.0))
    packed = (hi.astype(F32) + pltpu.roll(mid.astype(F32), SSM_HEADS, axis=1)
              + pltpu.roll(lo.astype(F32), 2 * SSM_HEADS, axis=1))
    return packed.astype(BF16)


def _ssd_kernel(xbc_ref, z_ref, dt_ref, x_ref, alog_ref, dx_ref, gn_ref, wo_ref,
                o_ref, state_ref, yn_ref, sel_ref, sel64_ref, *, ts):
    L = SSD_BLOCK
    GW = SSM_GROUP_WIDTH
    HPG = SSM_HEADS // SSM_GROUPS
    HD = SSM_HEAD_DIM
    b_off = SSM_INNER
    c_off = SSM_INNER + SSM_GROUPS * SSM_STATE

    @pl.when(pl.program_id(1) == 0)
    def _():
        state_ref[...] = jnp.zeros_like(state_ref)

    @pl.when((pl.program_id(0) == 0) & (pl.program_id(1) == 0))
    def _():
        for g in range(SSM_GROUPS):
            for ref, width in ((sel_ref, LANES), (sel64_ref, HD)):
                r = lax.broadcasted_iota(jnp.int32, (LANES, HPG * width), 0)
                e = lax.broadcasted_iota(jnp.int32, (LANES, HPG * width), 1) // width
                hit = (r < 3 * SSM_HEADS) & ((r % SSM_HEADS) == g * HPG + e)
                ref[g] = jnp.where(hit, 1.0, 0.0).astype(BF16)

    a_scale = -jnp.exp(alog_ref[...]) * LOG2E
    row = lax.broadcasted_iota(jnp.int32, (L, L), 0)
    col = lax.broadcasted_iota(jnp.int32, (L, L), 1)
    causal = row >= col
    tri = jnp.where(causal, 1.0, 0.0).astype(BF16)
    lane = lax.broadcasted_iota(jnp.int32, (L, LANES), 1)
    low = lane < HD
    valid = lane < SSM_HEADS
    lane_grp = lax.broadcasted_iota(jnp.int32, (L, GW), 1) // HD

    def block(c):
        r0 = pl.multiple_of(c * L, L)
        rows = pl.ds(r0, L)
        dt = dt_ref[rows, :]
        a = dt * a_scale
        acum = None
        for term in _split3(a):
            part = jnp.dot(tri, term, preferred_element_type=F32)
            acum = part if acum is None else acum + part
        acum_t = jnp.transpose(acum)
        acum_p = _pack3(acum, valid)
        dt_p = _pack3(dt, valid)

        for g in range(SSM_GROUPS):
            h0 = g * HPG
            acol = jnp.dot(acum_p, sel_ref[g], preferred_element_type=F32)
            dt_x = jnp.dot(dt_p, sel64_ref[g], preferred_element_type=F32)
            ab = [acol[:, e * LANES:(e + 1) * LANES] for e in range(HPG)]
            acum_x = jnp.concatenate([jnp.where(low, ab[0], ab[1]), jnp.where(low, ab[2], ab[3])], axis=1)
            alast_x = acum_x[L - 1:L, :]

            xs = xbc_ref[rows, g * GW:(g + 1) * GW].astype(F32)
            bg = xbc_ref[rows, b_off + g * SSM_STATE:b_off + (g + 1) * SSM_STATE]
            cg = xbc_ref[rows, c_off + g * SSM_STATE:c_off + (g + 1) * SSM_STATE]
            bgt = jnp.transpose(bg.astype(F32)).astype(BF16)
            cb = jnp.dot(cg, bgt, preferred_element_type=F32)

            xd = xs * dt_x
            xd_b = xd.astype(BF16)
            ws, xparts = [], []
            for e in range(HPG):
                xparts.append(jnp.where(lane_grp == e, xd_b, jnp.zeros_like(xd_b)))
                seg = ab[e] - acum_t[h0 + e:h0 + e + 1, :]
                ws.append((cb * jnp.exp2(jnp.where(causal, seg, -jnp.inf))).astype(BF16))
            y = jnp.dot(jnp.concatenate(ws, axis=1), jnp.concatenate(xparts, axis=0),
                        preferred_element_type=F32)

            st = state_ref[g]
            y = y + jnp.dot(cg, st.astype(BF16), preferred_element_type=F32) * jnp.exp2(acum_x)
            to_end = (xd * jnp.exp2(alast_x - acum_x)).astype(BF16)
            state_ref[g] = st * jnp.exp2(alast_x) + jnp.dot(bgt, to_end, preferred_element_type=F32)

            y = y + xs * dx_ref[:, g * GW:(g + 1) * GW]
            y = y * _silu(z_ref[rows, g * GW:(g + 1) * GW].astype(F32))
            yn = y * _rms_scale(y) * gn_ref[:, g * GW:(g + 1) * GW]
            yn_ref[rows, g * GW:(g + 1) * GW] = yn.astype(BF16)

    def step(c, carry):
        block(c)
        return carry

    lax.fori_loop(0, ts // L, step, 0)
    o_ref[...] = x_ref[...] + jnp.dot(yn_ref[...], wo_ref[...], preferred_element_type=F32)


def _ssd(xbc, z, dt, x2, alog, dx, gn, wo, *, batch, seq, ts=512):
    nst = seq // ts
    row = lambda b, i: (b * nst + i, 0)
    t = batch * seq
    return pl.pallas_call(
        functools.partial(_ssd_kernel, ts=ts),
        grid=(batch, nst),
        in_specs=[
            pl.BlockSpec((ts, SSM_XBC), row),
            pl.BlockSpec((ts, SSM_INNER), row),
            pl.BlockSpec((ts, LANES), row),
            pl.BlockSpec((ts, D_MODEL), row),
            _const_spec((1, LANES)),
            _const_spec((1, SSM_INNER)),
            _const_spec((1, SSM_INNER)),
            _const_spec((SSM_INNER, D_MODEL)),
        ],
        out_specs=pl.BlockSpec((ts, D_MODEL), row),
        out_shape=jax.ShapeDtypeStruct((t, D_MODEL), F32),
        scratch_shapes=[
            pltpu.VMEM((SSM_GROUPS, SSM_STATE, SSM_GROUP_WIDTH), F32),
            pltpu.VMEM((ts, SSM_INNER), BF16),
            pltpu.VMEM((SSM_GROUPS, LANES, (SSM_HEADS // SSM_GROUPS) * LANES), BF16),
            pltpu.VMEM((SSM_GROUPS, LANES, SSM_GROUP_WIDTH), BF16),
        ],
        compiler_params=_params(("arbitrary", "arbitrary")),
        name="ssd_out",
    )(xbc, z, dt, x2, alog, dx, gn, wo)


_FFN_CHUNKS = ((0, 512), (512, 512), (1024, 512), (1536, 512), (2048, 512), (2560, 256))


def _ffn_kernel(x_ref, g_ref, win_ref, cw_ref, cb_ref, wd_ref, o_ref, hist_ref, hid_ref):
    @pl.when(pl.program_id(1) == 0)
    def _():
        hist_ref[...] = jnp.zeros_like(hist_ref)

    x = x_ref[...]
    h = (x * _rms_scale(x) * g_ref[...]).astype(BF16)
    for c, width in _FFN_CHUNKS:
        gate = jnp.dot(h, win_ref[:, c:c + width], preferred_element_type=F32)
        val = jnp.dot(h, win_ref[:, FFN_DIM + c:FFN_DIM + c + width], preferred_element_type=F32)
        conv = _causal_conv(gate, hist_ref, cw_ref, cb_ref, c, width, FFN_CONV)
        hid_ref[:, c:c + width] = (_silu(conv) * val).astype(BF16)
    o_ref[...] = x + jnp.dot(hid_ref[...], wd_ref[...], preferred_element_type=F32)


def _ffn(x2, g, win, cw, cb, wd, *, batch, seq, tm=512):
    nst = seq // tm
    row = lambda b, i: (b * nst + i, 0)
    t = batch * seq
    return pl.pallas_call(
        _ffn_kernel,
        grid=(batch, nst),
        in_specs=[
            pl.BlockSpec((tm, D_MODEL), row),
            _const_spec((1, D_MODEL)),
            _const_spec((D_MODEL, 2 * FFN_DIM)),
            _const_spec((FFN_CONV, FFN_DIM)),
            _const_spec((1, FFN_DIM)),
            _const_spec((FFN_DIM, D_MODEL)),
        ],
        out_specs=pl.BlockSpec((tm, D_MODEL), row),
        out_shape=jax.ShapeDtypeStruct((t, D_MODEL), F32),
        scratch_shapes=[
            pltpu.VMEM((CARRY_ROWS, FFN_DIM), F32),
            pltpu.VMEM((tm, FFN_DIM), BF16),
        ],
        compiler_params=_params(("arbitrary", "arbitrary")),
        name="conv_ffn",
    )(x2, g, win, cw, cb, wd)


def _rope_kernel(pos_ref, invf_ref, cos_ref, sin_ref):
    ang = pos_ref[...].astype(F32) * invf_ref[...]
    cos_ref[...] = jnp.cos(ang)
    sin_ref[...] = jnp.sin(ang)


def _rope_tables(pos_col, invf, *, tm=1024):
    t = pos_col.shape[0]
    return pl.pallas_call(
        _rope_kernel,
        grid=(t // tm,),
        in_specs=[pl.BlockSpec((tm, 1), lambda i: (i, 0)), _const_spec((1, LANES))],
        out_specs=[pl.BlockSpec((tm, LANES), lambda i: (i, 0))] * 2,
        out_shape=[jax.ShapeDtypeStruct((t, LANES), F32)] * 2,
        compiler_params=_params(("arbitrary",)),
        name="rope_tables",
    )(pos_col, invf)


def _head_norm_rope(y, gain_ref, cos, sin_signed, first_half, ones_bd, scale):
    outs = []
    w = y.shape[1]
    for c in range(0, w, 2 * LANES):
        blk = y[:, c:c + 2 * LANES]
        ss = jnp.dot((blk * blk).astype(BF16), ones_bd, preferred_element_type=F32)
        yn = blk * lax.rsqrt(ss * (1.0 / ATT_HEAD_DIM) + EPS) * gain_ref[:, c:c + 2 * LANES]
        for j in range(0, 2 * LANES, LANES):
            v = yn[:, j:j + LANES]
            lo = pltpu.roll(v, ATT_HEAD_DIM // 2, axis=1)
            hi = pltpu.roll(v, LANES - ATT_HEAD_DIM // 2, axis=1)
            rot = jnp.where(first_half, hi, lo)
            outs.append((v * cos + rot * sin_signed) * scale)
    return outs


def _dup_heads(blocks, low):
    outs = []
    for v in blocks:
        swapped = pltpu.roll(v, ATT_HEAD_DIM, axis=1)
        outs.append(jnp.where(low, v, swapped).astype(BF16))
        outs.append(jnp.where(low, swapped, v).astype(BF16))
    return jnp.concatenate(outs, axis=1)


def _qkv_kernel(x_ref, cos_ref, sin_ref, gkv_ref, gq_ref, wkv_ref, bkv_ref, wq_ref, bq_ref,
                kn_ref, qn_ref, q_ref, k_ref, v_ref, *, tm):
    x = x_ref[...]
    xn = x * _rms_scale(x)
    hkv = (xn * gkv_ref[...]).astype(BF16)
    hq = (xn * gq_ref[...]).astype(BF16)
    kw = ATT_KV_HEADS * ATT_HEAD_DIM

    lane = lax.broadcasted_iota(jnp.int32, (tm, LANES), 1)
    first_half = (lane % ATT_HEAD_DIM) < (ATT_HEAD_DIM // 2)
    low = lane < ATT_HEAD_DIM
    cos = cos_ref[...]
    sin_signed = jnp.where(first_half, -sin_ref[...], sin_ref[...])
    r = lax.broadcasted_iota(jnp.int32, (2 * LANES, 2 * LANES), 0) // ATT_HEAD_DIM
    c = lax.broadcasted_iota(jnp.int32, (2 * LANES, 2 * LANES), 1) // ATT_HEAD_DIM
    ones_bd = jnp.where(r == c, 1.0, 0.0).astype(BF16)

    kv = jnp.dot(hkv, wkv_ref[...], preferred_element_type=F32) + bkv_ref[...]
    k_ref[...] = _dup_heads(_head_norm_rope(kv[:, :kw], kn_ref, cos, sin_signed, first_half, ones_bd, 1.0), low)
    v_ref[...] = _dup_heads([kv[:, kw + j:kw + j + LANES] for j in range(0, kw, LANES)], low)
    q = jnp.dot(hq, wq_ref[...], preferred_element_type=F32) + bq_ref[...]
    qs = _head_norm_rope(q, qn_ref, cos, sin_signed, first_half, ones_bd, ATT_HEAD_DIM ** -0.5)
    q_ref[...] = jnp.concatenate([v.astype(BF16) for v in qs], axis=1)


def _qkv(x2, cos, sin, gkv, gq, wkv, bkv, wq, bq, kn, qn, *, tm=512):
    t = x2.shape[0]
    kw = ATT_KV_HEADS * ATT_HEAD_DIM
    qw = ATT_Q_HEADS * ATT_HEAD_DIM
    row = lambda i: (i, 0)
    return pl.pallas_call(
        functools.partial(_qkv_kernel, tm=tm),
        grid=(t // tm,),
        in_specs=[
            pl.BlockSpec((tm, D_MODEL), row),
            pl.BlockSpec((tm, LANES), row),
            pl.BlockSpec((tm, LANES), row),
            _const_spec((1, D_MODEL)),
            _const_spec((1, D_MODEL)),
            _const_spec((D_MODEL, 2 * kw)),
            _const_spec((1, 2 * kw)),
            _const_spec((D_MODEL, qw)),
            _const_spec((1, qw)),
            _const_spec((1, kw)),
            _const_spec((1, qw)),
        ],
        out_specs=[
            pl.BlockSpec((tm, qw), row),
            pl.BlockSpec((tm, 2 * kw), row),
            pl.BlockSpec((tm, 2 * kw), row),
        ],
        out_shape=[
            jax.ShapeDtypeStruct((t, qw), BF16),
            jax.ShapeDtypeStruct((t, 2 * kw), BF16),
            jax.ShapeDtypeStruct((t, 2 * kw), BF16),
        ],
        compiler_params=_params(("arbitrary",)),
        name="qkv_rope",
    )(x2, cos, sin, gkv, gq, wkv, bkv, wq, bq, kn, qn)


def _attn_kernel(sink_ref, q_ref, k_ref, v_ref, x_ref, wo_ref, bo_ref, o_ref, att_ref, *, tq):
    i = pl.program_id(1)
    W = WINDOW
    G = ATT_GROUP
    low = lax.broadcasted_iota(jnp.int32, (W, LANES), 1) < ATT_HEAD_DIM
    qi = lax.broadcasted_iota(jnp.int32, (W, 2 * W), 0)
    ki = lax.broadcasted_iota(jnp.int32, (W, 2 * W), 1)
    ones = jnp.ones((2 * W, LANES), BF16)

    for jb in range(tq // W):
        q0 = i * tq + jb * W
        start = pl.multiple_of(jnp.maximum(q0 - W, 0), W)
        rel = qi + (q0 - start) - ki
        neg = jnp.where((rel >= 0) & (rel < W), 0.0, -jnp.inf)
        rows = slice(jb * W, (jb + 1) * W)
        for hk in range(ATT_KV_HEADS):
            kh = k_ref[pl.ds(start, 2 * W), hk * LANES:(hk + 1) * LANES]
            vh = v_ref[pl.ds(start, 2 * W), hk * LANES:(hk + 1) * LANES]
            parts = []
            for g in range(G):
                qb = q_ref[rows, (hk * G + g) // 2 * LANES:((hk * G + g) // 2 + 1) * LANES]
                keep = low if g % 2 == 0 else jnp.logical_not(low)
                parts.append(jnp.where(keep, qb, jnp.zeros_like(qb)))
            qst = jnp.concatenate(parts, axis=0)
            s = lax.dot_general(qst, kh, (((1,), (1,)), ((), ())), preferred_element_type=F32)
            ps, ms = [], []
            for g in range(G):
                sg = s[g * W:(g + 1) * W, :] + neg
                m = jnp.maximum(jnp.max(sg, axis=-1, keepdims=True), sink_ref[hk * G + g])
                ps.append(jnp.exp(sg - m).astype(BF16))
                ms.append(m)
            pv = jnp.dot(jnp.concatenate(ps, axis=0), jnp.concatenate([vh, ones], axis=1),
                         preferred_element_type=F32)
            outs = []
            for g in range(G):
                blk = pv[g * W:(g + 1) * W, :]
                denom = blk[:, LANES:] + jnp.exp(sink_ref[hk * G + g] - ms[g])
                outs.append(blk[:, :LANES] / denom)
            for half in range(G // 2):
                col = (hk * G) // 2 + half
                att_ref[rows, col * LANES:(col + 1) * LANES] = jnp.where(
                    low, outs[2 * half], outs[2 * half + 1]).astype(BF16)
    o_ref[...] = (x_ref[...] + bo_ref[...]
                  + jnp.dot(att_ref[...], wo_ref[...], preferred_element_type=F32))


def _attn(sinks, q, k, v, x2, wo, bo, *, batch, seq, tq=512):
    nst = seq // tq
    qw = ATT_Q_HEADS * ATT_HEAD_DIM
    kw = ATT_KV_HEADS * ATT_HEAD_DIM
    row = lambda b, i: (b * nst + i, 0)
    per_batch = lambda b, i: (b, 0)
    t = batch * seq
    return pl.pallas_call(
        functools.partial(_attn_kernel, tq=tq),
        grid=(batch, nst),
        in_specs=[
            pl.BlockSpec(memory_space=pltpu.SMEM),
            pl.BlockSpec((tq, qw), row),
            pl.BlockSpec((seq, 2 * kw), per_batch),
            pl.BlockSpec((seq, 2 * kw), per_batch),
            pl.BlockSpec((tq, D_MODEL), row),
            _const_spec((qw, D_MODEL)),
            _const_spec((1, D_MODEL)),
        ],
        out_specs=pl.BlockSpec((tq, D_MODEL), row),
        out_shape=jax.ShapeDtypeStruct((t, D_MODEL), F32),
        scratch_shapes=[pltpu.VMEM((tq, qw), BF16)],
        compiler_params=_params(("arbitrary", "arbitrary")),
        name="swa_attn",
    )(sinks, q, k, v, x2, wo, bo)


def _row(v, width=None):
    v = v.reshape(1, -1).astype(F32)
    if width is not None and v.shape[1] < width:
        v = jnp.pad(v, ((0, 0), (0, width - v.shape[1])))
    return v


def kernel(x, positions, a_norm, a_in_proj, a_conv_w, a_conv_b, a_dt_bias, a_A_log, a_D, a_gnorm,
           a_out_proj, kv_norm, w_kv, b_kv, k_norm, b_norm, w_q, b_q, q_norm, sinks, w_o, b_o,
           f_norm, f_w_in, f_conv_w, f_conv_b, f_w_down):
    batch, seq, _ = x.shape
    t = batch * seq
    x2 = x.reshape(t, D_MODEL)

    w_in = a_in_proj[0]
    w_main = w_in[:, :SSM_INNER + SSM_XBC].astype(BF16)
    w_dt = jnp.pad(w_in[:, SSM_INNER + SSM_XBC:], ((0, 0), (0, LANES - SSM_HEADS))).astype(BF16)
    z, xbc, dt = _mamba_in(x2, _row(a_norm[0]), w_main, w_dt, a_conv_w[0].astype(F32),
                           _row(a_conv_b[0]), _row(a_dt_bias[0], LANES), batch=batch, seq=seq)
    dx = jnp.repeat(a_D[0].astype(F32), SSM_HEAD_DIM).reshape(1, SSM_INNER)
    x2 = _ssd(xbc, z, dt, x2, _row(a_A_log[0], LANES), dx, _row(a_gnorm[0]),
              a_out_proj[0].astype(BF16), batch=batch, seq=seq)
    x2 = _ffn(x2, _row(f_norm[0]), f_w_in[0].astype(BF16), f_conv_w[0].astype(F32),
              _row(f_conv_b[0]), f_w_down[0].astype(BF16), batch=batch, seq=seq)

    half = ATT_HEAD_DIM // 2
    inv_freq = ROPE_THETA ** (-jnp.arange(half, dtype=F32) / half)
    invf = jnp.tile(inv_freq, LANES // half).reshape(1, LANES)
    cos, sin = _rope_tables(positions.reshape(t, 1), invf)
    q, k, v = _qkv(x2, cos, sin, _row(kv_norm), _row(b_norm[0]), w_kv.astype(BF16), _row(b_kv),
                   w_q[0].astype(BF16), _row(b_q[0]),
                   _row(jnp.tile(k_norm, ATT_KV_HEADS)), _row(jnp.tile(q_norm[0], ATT_Q_HEADS)))
    x2 = _attn(sinks[0].astype(F32), q, k, v, x2, w_o[0].astype(BF16), _row(b_o[0]),
               batch=batch, seq=seq)
    x2 = _ffn(x2, _row(f_norm[1]), f_w_in[1].astype(BF16), f_conv_w[1].astype(F32),
              _row(f_conv_b[1]), f_w_down[1].astype(BF16), batch=batch, seq=seq)
    return x2.reshape(batch, seq, D_MODEL)
```

```python
import functools

import jax
import jax.numpy as jnp
from jax import lax
from jax.experimental import pallas as pl
from jax.experimental.pallas import tpu as pltpu

F32 = jnp.float32
BF16 = jnp.bfloat16

D_MODEL = 1024
EPS = 1e-5

SSM_INNER = 2048
SSM_HEAD_DIM = 64
SSM_HEADS = 32
SSM_GROUPS = 8
SSM_STATE = 128
SSM_CONV = 4
SSD_BLOCK = 128
LOG2E = 1.4426950408889634
SSD_GROUP_BATCH = 2
SSM_XBC = SSM_INNER + 2 * SSM_GROUPS * SSM_STATE
SSM_GROUP_WIDTH = SSM_INNER // SSM_GROUPS
SSM_PROJ = SSM_INNER + SSM_XBC + SSM_HEADS

ATT_HEAD_DIM = 64
ATT_Q_HEADS = 16
ATT_KV_HEADS = 4
ATT_GROUP = 4
WINDOW = 128
ROPE_THETA = 10000.0

FFN_DIM = 2816
FFN_CONV = 3

LANES = 128
CARRY_ROWS = 8
VMEM_LIMIT = 56 * 1024 * 1024


def _params(sem):
    return pltpu.CompilerParams(dimension_semantics=sem, vmem_limit_bytes=VMEM_LIMIT)


def _const_spec(shape):
    nd = len(shape)
    return pl.BlockSpec(shape, lambda *_: (0,) * nd, pipeline_mode=pl.Buffered(1))


def _layer_spec(shape, layer):
    nd = len(shape)
    return pl.BlockSpec((None,) + tuple(shape), lambda *_: (layer,) + (0,) * nd, pipeline_mode=pl.Buffered(1))


def _rms_scale(x):
    return lax.rsqrt(jnp.mean(x * x, axis=-1, keepdims=True) + EPS)


def _silu_of_half(hx):
    return hx + hx * jnp.tanh(hx)


def _causal_conv(pre, hist_ref, w_ref, b_ref, c0, width, taps):
    tm = pre.shape[0]
    cols = slice(c0, c0 + width)
    ext = jnp.concatenate([hist_ref[:, cols], pre], axis=0)
    hist_ref[:, cols] = pre[tm - CARRY_ROWS:tm, :]
    w = [w_ref[k:k + 1, cols] for k in range(taps)]
    back1 = pltpu.roll(ext, 1, axis=0)
    if taps == 4:
        near = w[3] * ext + w[2] * back1
        far = w[1] * ext + w[0] * back1
        out = near + pltpu.roll(far, 2, axis=0)
    else:
        assert taps == 3
        far = w[1] * ext + w[0] * back1
        out = w[2] * ext + pltpu.roll(far, 1, axis=0)
    return out[CARRY_ROWS:, :] + b_ref[:, cols]


def _mamba_in_kernel(x_ref, g_ref, w_ref, wdt_ref, cw_ref, cb_ref, dtb_ref,
                     z_ref, xbc_ref, dt_ref, hist_ref, *, tm, chunk):
    @pl.when(pl.program_id(1) == 0)
    def _():
        hist_ref[...] = jnp.zeros_like(hist_ref)

    x = x_ref[...]
    h = (x * _rms_scale(x) * g_ref[...]).astype(BF16)
    z_chunks = list(range(0, SSM_INNER, chunk))
    for i, c in enumerate(range(0, SSM_XBC, chunk)):
        pre = jnp.dot(h, w_ref[:, SSM_INNER + c:SSM_INNER + c + chunk], preferred_element_type=F32)
        conv = _causal_conv(pre, hist_ref, cw_ref, cb_ref, c, chunk, SSM_CONV)
        xbc_ref[:, c:c + chunk] = _silu_of_half(conv.astype(BF16))
        if i % 2 == 1:
            zc = z_chunks[i // 2]
            z_ref[:, zc:zc + chunk] = jnp.dot(h, w_ref[:, zc:zc + chunk], preferred_element_type=F32).astype(BF16)
    dtp = jnp.dot(h, wdt_ref[...], preferred_element_type=F32) + dtb_ref[...]
    dt_ref[...] = jnp.maximum(dtp, 0.0) + jnp.log1p(jnp.exp(-jnp.abs(dtp)))


def _mamba_in(x2, g, w, wdt, cw, cb, dtb, *, batch, seq, tm=512, chunk=256):
    nst = seq // tm
    row = lambda b, i: (b * nst + i, 0)
    t = batch * seq
    return pl.pallas_call(
        functools.partial(_mamba_in_kernel, tm=tm, chunk=chunk),
        grid=(batch, nst),
        in_specs=[
            pl.BlockSpec((tm, D_MODEL), row),
            _const_spec((1, D_MODEL)),
            _const_spec((D_MODEL, SSM_PROJ)),
            _const_spec((D_MODEL, LANES)),
            _const_spec((SSM_CONV, SSM_XBC)),
            _const_spec((1, SSM_XBC)),
            _const_spec((1, LANES)),
        ],
        out_specs=[
            pl.BlockSpec((tm, SSM_INNER), row),
            pl.BlockSpec((tm, SSM_XBC), row),
            pl.BlockSpec((tm, LANES), row),
        ],
        out_shape=[
            jax.ShapeDtypeStruct((t, SSM_INNER), BF16),
            jax.ShapeDtypeStruct((t, SSM_XBC), BF16),
            jax.ShapeDtypeStruct((t, LANES), F32),
        ],
        scratch_shapes=[pltpu.VMEM((CARRY_ROWS, SSM_XBC), F32)],
        compiler_params=_params(("arbitrary", "arbitrary")),
        name="mamba_in",
    )(x2, g, w, wdt, cw, cb, dtb)


def _split3(a):
    hi = a.astype(BF16)
    r1 = a - hi.astype(F32)
    mid = r1.astype(BF16)
    lo = (r1 - mid.astype(F32)).astype(BF16)
    return hi, mid, lo


def _pack3(v, valid):
    hi, mid, lo = _split3(jnp.where(valid, v, 0.0))
    packed = (hi.astype(F32) + pltpu.roll(mid.astype(F32), SSM_HEADS, axis=1)
              + pltpu.roll(lo.astype(F32), 2 * SSM_HEADS, axis=1))
    return packed.astype(BF16)


def _ssd_kernel(xbc_ref, z_ref, dt_ref, x_ref, alog_ref, dx_ref, wo_ref,
                o_ref, state_ref, yn_ref, sel_ref, sel64_ref, *, ts):
    L = SSD_BLOCK
    GW = SSM_GROUP_WIDTH
    HPG = SSM_HEADS // SSM_GROUPS
    HD = SSM_HEAD_DIM
    b_off = SSM_INNER
    c_off = SSM_INNER + SSM_GROUPS * SSM_STATE

    @pl.when(pl.program_id(1) == 0)
    def _():
        state_ref[...] = jnp.zeros_like(state_ref)

    @pl.when((pl.program_id(0) == 0) & (pl.program_id(1) == 0))
    def _():
        for g in range(SSM_GROUPS):
            for ref, width in ((sel_ref, LANES), (sel64_ref, HD)):
                r = lax.broadcasted_iota(jnp.int32, (LANES, HPG * width), 0)
                e = lax.broadcasted_iota(jnp.int32, (LANES, HPG * width), 1) // width
                hit = (r < 3 * SSM_HEADS) & ((r % SSM_HEADS) == g * HPG + e)
                ref[g] = jnp.where(hit, 1.0, 0.0).astype(BF16)

    a_scale = -jnp.exp(alog_ref[...]) * LOG2E
    row = lax.broadcasted_iota(jnp.int32, (L, L), 0)
    col = lax.broadcasted_iota(jnp.int32, (L, L), 1)
    causal = row >= col
    tri = jnp.where(causal, 1.0, 0.0).astype(BF16)
    lane = lax.broadcasted_iota(jnp.int32, (L, LANES), 1)
    valid = lane < SSM_HEADS
    low = lane < HD
    lane_grp = lax.broadcasted_iota(jnp.int32, (L, GW), 1) // HD

    def block(c):
        r0 = pl.multiple_of(c * L, L)
        rows = pl.ds(r0, L)
        dt = dt_ref[rows, :]
        a = dt * a_scale
        acum = None
        for term in _split3(a):
            part = jnp.dot(tri, term, preferred_element_type=F32)
            acum = part if acum is None else acum + part
        acum_t = jnp.transpose(acum)
        acum_p = _pack3(acum, valid)
        dt_p = _pack3(dt, valid)

        for g0 in range(0, SSM_GROUPS, SSD_GROUP_BATCH):
            scan_groups(rows, acum_t, acum_p, dt_p, range(g0, g0 + SSD_GROUP_BATCH))

    def scan_groups(rows, acum_t, acum_p, dt_p, G):
        xs = {g: xbc_ref[rows, g * GW:(g + 1) * GW].astype(F32) for g in G}
        cg = {g: xbc_ref[rows, c_off + g * SSM_STATE:c_off + (g + 1) * SSM_STATE] for g in G}
        bgt = {g: jnp.transpose(xbc_ref[rows, b_off + g * SSM_STATE:b_off + (g + 1) * SSM_STATE]
                                .astype(F32)).astype(BF16) for g in G}
        acol = {g: jnp.dot(acum_p, sel_ref[g], preferred_element_type=F32) for g in G}
        dt_x = {g: jnp.dot(dt_p, sel64_ref[g], preferred_element_type=F32) for g in G}
        cb = {g: jnp.dot(cg[g], bgt[g], preferred_element_type=F32) for g in G}
        st = {g: state_ref[g] for g in G}
        y_in = {g: jnp.dot(cg[g], st[g].astype(BF16), preferred_element_type=F32) for g in G}

        xd, wcat, xcat, acum_x = {}, {}, {}, {}
        for g in G:
            ab = [acol[g][:, e * LANES:(e + 1) * LANES] for e in range(HPG)]
            acum_x[g] = jnp.concatenate([jnp.where(low, ab[0], ab[1]), jnp.where(low, ab[2], ab[3])], axis=1)
            xd[g] = xs[g] * dt_x[g]
            xd_b = xd[g].astype(BF16)
            ws, xparts = [], []
            for e in range(HPG):
                xparts.append(jnp.where(lane_grp == e, xd_b, jnp.zeros_like(xd_b)))
                seg = ab[e] - acum_t[g * HPG + e:g * HPG + e + 1, :]
                ws.append((cb[g] * jnp.exp2(jnp.where(causal, seg, -jnp.inf))).astype(BF16))
            wcat[g] = jnp.concatenate(ws, axis=1)
            xcat[g] = jnp.concatenate(xparts, axis=0)
        y = {g: jnp.dot(wcat[g], xcat[g], preferred_element_type=F32) for g in G}

        to_end, alast = {}, {}
        for g in G:
            alast[g] = acum_x[g][L - 1:L, :]
            alast_x = alast[g]
            y[g] = y[g] + y_in[g] * jnp.exp2(acum_x[g])
            to_end[g] = (xd[g] * jnp.exp2(alast_x - acum_x[g])).astype(BF16)
        for g in G:
            alast_x = alast[g]
            state_ref[g] = st[g] * jnp.exp2(alast_x) + jnp.dot(bgt[g], to_end[g], preferred_element_type=F32)
        for g in G:
            yg = y[g] + xs[g] * dx_ref[:, g * GW:(g + 1) * GW]
            yg = yg * _silu_of_half(0.5 * z_ref[rows, g * GW:(g + 1) * GW].astype(F32))
            yn = yg * _rms_scale(yg)
            yn_ref[rows, g * GW:(g + 1) * GW] = yn.astype(BF16)

    def step(c, carry):
        block(c)
        return carry

    lax.fori_loop(0, ts // L, step, 0)
    o_ref[...] = x_ref[...] + jnp.dot(yn_ref[...], wo_ref[...], preferred_element_type=F32)


def _ssd(xbc, z, dt, x2, alog, dx, wo, *, batch, seq, ts=512):
    nst = seq // ts
    row = lambda b, i: (b * nst + i, 0)
    t = batch * seq
    return pl.pallas_call(
        functools.partial(_ssd_kernel, ts=ts),
        grid=(batch, nst),
        in_specs=[
            pl.BlockSpec((ts, SSM_XBC), row),
            pl.BlockSpec((ts, SSM_INNER), row),
            pl.BlockSpec((ts, LANES), row),
            pl.BlockSpec((ts, D_MODEL), row),
            _const_spec((1, LANES)),
            _const_spec((1, SSM_INNER)),
            _const_spec((SSM_INNER, D_MODEL)),
        ],
        out_specs=pl.BlockSpec((ts, D_MODEL), row),
        out_shape=jax.ShapeDtypeStruct((t, D_MODEL), F32),
        scratch_shapes=[
            pltpu.VMEM((SSM_GROUPS, SSM_STATE, SSM_GROUP_WIDTH), F32),
            pltpu.VMEM((ts, SSM_INNER), BF16),
            pltpu.VMEM((SSM_GROUPS, LANES, (SSM_HEADS // SSM_GROUPS) * LANES), BF16),
            pltpu.VMEM((SSM_GROUPS, LANES, SSM_GROUP_WIDTH), BF16),
        ],
        compiler_params=_params(("arbitrary", "arbitrary")),
        name="ssd_out",
    )(xbc, z, dt, x2, alog, dx, wo)


_FFN_CHUNKS = ((0, 512), (512, 512), (1024, 512), (1536, 512), (2048, 512), (2560, 256))


def _ffn_kernel(x_ref, g_ref, win_ref, cw_ref, cb_ref, wd_ref, o_ref, hist_ref, hid_ref):
    @pl.when(pl.program_id(1) == 0)
    def _():
        hist_ref[...] = jnp.zeros_like(hist_ref)

    x = x_ref[...]
    h = (x * _rms_scale(x) * g_ref[...]).astype(BF16)
    for c, width in _FFN_CHUNKS:
        gate = jnp.dot(h, win_ref[:, c:c + width], preferred_element_type=F32)
        val = jnp.dot(h, win_ref[:, FFN_DIM + c:FFN_DIM + c + width], preferred_element_type=F32)
        conv = _causal_conv(gate, hist_ref, cw_ref, cb_ref, c, width, FFN_CONV)
        hid_ref[:, c:c + width] = (_silu_of_half(conv) * val).astype(BF16)
    o_ref[...] = x + jnp.dot(hid_ref[...], wd_ref[...], preferred_element_type=F32)


def _ffn(x2, g, win, cw, cb, wd, *, layer, batch, seq, tm=512):
    nst = seq // tm
    row = lambda b, i: (b * nst + i, 0)
    t = batch * seq
    return pl.pallas_call(
        _ffn_kernel,
        grid=(batch, nst),
        in_specs=[
            pl.BlockSpec((tm, D_MODEL), row),
            _const_spec((1, D_MODEL)),
            _layer_spec((D_MODEL, 2 * FFN_DIM), layer),
            _const_spec((FFN_CONV, FFN_DIM)),
            _const_spec((1, FFN_DIM)),
            _layer_spec((FFN_DIM, D_MODEL), layer),
        ],
        out_specs=pl.BlockSpec((tm, D_MODEL), row),
        out_shape=jax.ShapeDtypeStruct((t, D_MODEL), F32),
        scratch_shapes=[
            pltpu.VMEM((CARRY_ROWS, FFN_DIM), F32),
            pltpu.VMEM((tm, FFN_DIM), BF16),
        ],
        compiler_params=_params(("arbitrary", "arbitrary")),
        name="conv_ffn",
    )(x2, g, win, cw, cb, wd)


def _rope_kernel(pos_ref, invf_ref, cos_ref, sin_ref):
    ang = pos_ref[...].astype(F32) * invf_ref[...]
    cos_ref[...] = jnp.cos(ang)
    sin_ref[...] = jnp.sin(ang)


def _rope_tables(pos_packed, invf, *, tm=1024):
    rows = pos_packed.shape[0]
    tm = min(tm, rows)
    return pl.pallas_call(
        _rope_kernel,
        grid=(rows // tm,),
        in_specs=[pl.BlockSpec((tm, LANES), lambda i: (i, 0)), _const_spec((1, LANES))],
        out_specs=[pl.BlockSpec((tm, LANES), lambda i: (i, 0))] * 2,
        out_shape=[jax.ShapeDtypeStruct((rows, LANES), F32)] * 2,
        compiler_params=_params(("arbitrary",)),
        name="rope_tables",
    )(pos_packed, invf)


def _head_norm_rope(y, gain_ref, cos, sin_signed, first_half, ones_bd):
    outs = []
    w = y.shape[1]
    for c in range(0, w, 2 * LANES):
        blk = y[:, c:c + 2 * LANES]
        ss = jnp.dot((blk * blk).astype(BF16), ones_bd, preferred_element_type=F32)
        yn = blk * lax.rsqrt(ss * (1.0 / ATT_HEAD_DIM) + EPS) * gain_ref[:, c:c + 2 * LANES]
        for j in range(0, 2 * LANES, LANES):
            v = yn[:, j:j + LANES]
            lo = pltpu.roll(v, ATT_HEAD_DIM // 2, axis=1)
            hi = pltpu.roll(v, LANES - ATT_HEAD_DIM // 2, axis=1)
            rot = jnp.where(first_half, hi, lo)
            outs.append(v * cos + rot * sin_signed)
    return outs


def _dup_heads(blocks, low):
    outs = []
    for v in blocks:
        swapped = pltpu.roll(v, ATT_HEAD_DIM, axis=1)
        outs.append(jnp.where(low, v, swapped).astype(BF16))
        outs.append(jnp.where(low, swapped, v).astype(BF16))
    return jnp.concatenate(outs, axis=1)


def _qkv_kernel(x_ref, cos_ref, sin_ref, gkv_ref, gq_ref, wkv_ref, bkv_ref, wq_ref, bq_ref,
                kn_ref, qn_ref, q_ref, k_ref, v_ref, *, tm):
    x = x_ref[...]
    xn = x * _rms_scale(x)
    hkv = (xn * gkv_ref[...]).astype(BF16)
    hq = (xn * gq_ref[...]).astype(BF16)
    kw = ATT_KV_HEADS * ATT_HEAD_DIM

    lane = lax.broadcasted_iota(jnp.int32, (tm, LANES), 1)
    first_half = (lane % ATT_HEAD_DIM) < (ATT_HEAD_DIM // 2)
    low = lane < ATT_HEAD_DIM
    cos = cos_ref[...]
    sin_signed = jnp.where(first_half, -sin_ref[...], sin_ref[...])
    r = lax.broadcasted_iota(jnp.int32, (2 * LANES, 2 * LANES), 0) // ATT_HEAD_DIM
    c = lax.broadcasted_iota(jnp.int32, (2 * LANES, 2 * LANES), 1) // ATT_HEAD_DIM
    ones_bd = jnp.where(r == c, 1.0, 0.0).astype(BF16)

    kv = jnp.dot(hkv, wkv_ref[...], preferred_element_type=F32) + bkv_ref[...]
    k_ref[...] = _dup_heads(_head_norm_rope(kv[:, :kw], kn_ref, cos, sin_signed, first_half, ones_bd), low)
    v_ref[...] = _dup_heads([kv[:, kw + j:kw + j + LANES] for j in range(0, kw, LANES)], low)
    q = jnp.dot(hq, wq_ref[...], preferred_element_type=F32) + bq_ref[...]
    qs = _head_norm_rope(q, qn_ref, cos, sin_signed, first_half, ones_bd)
    q_ref[...] = jnp.concatenate([v.astype(BF16) for v in qs], axis=1)


def _qkv(x2, cos, sin, gkv, gq, wkv, bkv, wq, bq, kn, qn, *, tm=512):
    t = x2.shape[0]
    kw = ATT_KV_HEADS * ATT_HEAD_DIM
    qw = ATT_Q_HEADS * ATT_HEAD_DIM
    row = lambda i: (i, 0)
    return pl.pallas_call(
        functools.partial(_qkv_kernel, tm=tm),
        grid=(t // tm,),
        in_specs=[
            pl.BlockSpec((tm, D_MODEL), row),
            pl.BlockSpec((tm, LANES), row),
            pl.BlockSpec((tm, LANES), row),
            _const_spec((1, D_MODEL)),
            _const_spec((1, D_MODEL)),
            _const_spec((D_MODEL, 2 * kw)),
            _const_spec((1, 2 * kw)),
            _const_spec((D_MODEL, qw)),
            _const_spec((1, qw)),
            _const_spec((1, kw)),
            _const_spec((1, qw)),
        ],
        out_specs=[
            pl.BlockSpec((tm, qw), row),
            pl.BlockSpec((tm, 2 * kw), row),
            pl.BlockSpec((tm, 2 * kw), row),
        ],
        out_shape=[
            jax.ShapeDtypeStruct((t, qw), BF16),
            jax.ShapeDtypeStruct((t, 2 * kw), BF16),
            jax.ShapeDtypeStruct((t, 2 * kw), BF16),
        ],
        compiler_params=_params(("arbitrary",)),
        name="qkv_rope",
    )(x2, cos, sin, gkv, gq, wkv, bkv, wq, bq, kn, qn)


def _attn_kernel(sink_ref, q_ref, k_ref, v_ref, x_ref, wo_ref, bo_ref, o_ref, att_ref, *, tq):
    i = pl.program_id(1)
    W = WINDOW
    G = ATT_GROUP
    low = lax.broadcasted_iota(jnp.int32, (W, LANES), 1) < ATT_HEAD_DIM
    qi = lax.broadcasted_iota(jnp.int32, (W, 2 * W), 0)
    ki = lax.broadcasted_iota(jnp.int32, (W, 2 * W), 1)
    ones = jnp.ones((2 * W, LANES), BF16)

    for jb in range(tq // W):
        q0 = i * tq + jb * W
        start = pl.multiple_of(jnp.maximum(q0 - W, 0), W)
        rel = qi + (q0 - start) - ki
        neg = jnp.where((rel >= 0) & (rel < W), 0.0, -jnp.inf)
        rows = slice(jb * W, (jb + 1) * W)
        for hk in range(ATT_KV_HEADS):
            kh = k_ref[pl.ds(start, 2 * W), hk * LANES:(hk + 1) * LANES]
            vh = v_ref[pl.ds(start, 2 * W), hk * LANES:(hk + 1) * LANES]
            parts = []
            for g in range(G):
                qb = q_ref[rows, (hk * G + g) // 2 * LANES:((hk * G + g) // 2 + 1) * LANES]
                keep = low if g % 2 == 0 else jnp.logical_not(low)
                parts.append(jnp.where(keep, qb, jnp.zeros_like(qb)))
            qst = jnp.concatenate(parts, axis=0)
            s = lax.dot_general(qst, kh, (((1,), (1,)), ((), ())), preferred_element_type=F32)
            ps, ms = [], []
            for g in range(G):
                sg = s[g * W:(g + 1) * W, :] + neg
                m = jnp.maximum(jnp.max(sg, axis=-1, keepdims=True), sink_ref[hk * G + g])
                ps.append(jnp.exp(sg - m).astype(BF16))
                ms.append(m)
            pv = jnp.dot(jnp.concatenate(ps, axis=0), jnp.concatenate([vh, ones], axis=1),
                         preferred_element_type=F32)
            outs = []
            for g in range(G):
                blk = pv[g * W:(g + 1) * W, :]
                denom = blk[:, LANES:] + jnp.exp(sink_ref[hk * G + g] - ms[g])
                outs.append(blk[:, :LANES] / denom)
            for half in range(G // 2):
                col = (hk * G) // 2 + half
                att_ref[rows, col * LANES:(col + 1) * LANES] = jnp.where(
                    low, outs[2 * half], outs[2 * half + 1]).astype(BF16)
    o_ref[...] = (x_ref[...] + bo_ref[...]
                  + jnp.dot(att_ref[...], wo_ref[...], preferred_element_type=F32))


def _attn(sinks, q, k, v, x2, wo, bo, *, batch, seq, tq=512):
    nst = seq // tq
    qw = ATT_Q_HEADS * ATT_HEAD_DIM
    kw = ATT_KV_HEADS * ATT_HEAD_DIM
    row = lambda b, i: (b * nst + i, 0)
    per_batch = lambda b, i: (b, 0)
    t = batch * seq
    return pl.pallas_call(
        functools.partial(_attn_kernel, tq=tq),
        grid=(batch, nst),
        in_specs=[
            pl.BlockSpec(memory_space=pltpu.SMEM),
            pl.BlockSpec((tq, qw), row),
            pl.BlockSpec((seq, 2 * kw), per_batch),
            pl.BlockSpec((seq, 2 * kw), per_batch),
            pl.BlockSpec((tq, D_MODEL), row),
            _const_spec((qw, D_MODEL)),
            _const_spec((1, D_MODEL)),
        ],
        out_specs=pl.BlockSpec((tq, D_MODEL), row),
        out_shape=jax.ShapeDtypeStruct((t, D_MODEL), F32),
        scratch_shapes=[pltpu.VMEM((tq, qw), BF16)],
        compiler_params=_params(("arbitrary", "arbitrary")),
        name="swa_attn",
    )(sinks, q, k, v, x2, wo, bo)


def _row(v, width=None):
    v = v.reshape(1, -1).astype(F32)
    if width is not None and v.shape[1] < width:
        v = jnp.pad(v, ((0, 0), (0, width - v.shape[1])))
    return v


def kernel(x, positions, a_norm, a_in_proj, a_conv_w, a_conv_b, a_dt_bias, a_A_log, a_D, a_gnorm,
           a_out_proj, kv_norm, w_kv, b_kv, k_norm, b_norm, w_q, b_q, q_norm, sinks, w_o, b_o,
           f_norm, f_w_in, f_conv_w, f_conv_b, f_w_down):
    batch, seq, _ = x.shape
    t = batch * seq
    x2 = x.reshape(t, D_MODEL)

    w_in = a_in_proj.reshape(D_MODEL, SSM_PROJ)
    w_dt = jnp.pad(w_in[:, SSM_INNER + SSM_XBC:], ((0, 0), (0, LANES - SSM_HEADS))).astype(BF16)
    z, xbc, dt = _mamba_in(x2, _row(a_norm[0]), w_in.astype(BF16), w_dt, 0.5 * a_conv_w[0].astype(F32),
                           0.5 * _row(a_conv_b[0]), _row(a_dt_bias[0], LANES), batch=batch, seq=seq)
    dx = jnp.repeat(a_D[0].astype(F32), SSM_HEAD_DIM).reshape(1, SSM_INNER)
    wo_ssm = (a_gnorm[0].astype(F32)[:, None] * a_out_proj.reshape(SSM_INNER, D_MODEL)).astype(BF16)
    x2 = _ssd(xbc, z, dt, x2, _row(a_A_log[0], LANES), dx, wo_ssm, batch=batch, seq=seq)
    f_win = f_w_in.astype(BF16)
    f_wd = f_w_down.astype(BF16)
    f_cw = 0.5 * f_conv_w.astype(F32)
    f_cb = 0.5 * f_conv_b.astype(F32)
    x2 = _ffn(x2, _row(f_norm[0]), f_win, f_cw[0], _row(f_cb[0]), f_wd, layer=0, batch=batch, seq=seq)

    half = ATT_HEAD_DIM // 2
    inv_freq = ROPE_THETA ** (-jnp.arange(half, dtype=F32) / half)
    invf = jnp.tile(inv_freq, LANES // half).reshape(1, LANES)
    per_row = LANES // half
    pos_packed = jnp.repeat(positions.reshape(t // per_row, per_row), half, axis=1)
    cos_p, sin_p = _rope_tables(pos_packed, invf)
    cos = jnp.tile(cos_p.reshape(t, half), (1, per_row))
    sin = jnp.tile(sin_p.reshape(t, half), (1, per_row))
    q_gain = jnp.tile(q_norm[0].astype(F32), ATT_Q_HEADS) * (ATT_HEAD_DIM ** -0.5)
    q, k, v = _qkv(x2, cos, sin, _row(kv_norm), _row(b_norm[0]), w_kv.astype(BF16), _row(b_kv),
                   w_q.reshape(D_MODEL, -1).astype(BF16), _row(b_q[0]),
                   _row(jnp.tile(k_norm, ATT_KV_HEADS)), _row(q_gain))
    x2 = _attn(sinks[0].astype(F32), q, k, v, x2, w_o.reshape(-1, D_MODEL).astype(BF16), _row(b_o[0]),
               batch=batch, seq=seq)
    x2 = _ffn(x2, _row(f_norm[1]), f_win, f_cw[1], _row(f_cb[1]), f_wd, layer=1, batch=batch, seq=seq)
    return x2.reshape(batch, seq, D_MODEL)
```

```python
import functools

import jax
import jax.numpy as jnp
from jax import lax
from jax.experimental import pallas as pl
from jax.experimental.pallas import tpu as pltpu

F32 = jnp.float32
BF16 = jnp.bfloat16

D_MODEL = 1024
EPS = 1e-5

SSM_INNER = 2048
SSM_HEAD_DIM = 64
SSM_HEADS = 32
SSM_GROUPS = 8
SSM_STATE = 128
SSM_CONV = 4
SSD_BLOCK = 128
LOG2E = 1.4426950408889634
SSD_GROUP_BATCH = 2
SSM_XBC = SSM_INNER + 2 * SSM_GROUPS * SSM_STATE
SSM_GROUP_WIDTH = SSM_INNER // SSM_GROUPS
SSM_PROJ = SSM_INNER + SSM_XBC + SSM_HEADS

ATT_HEAD_DIM = 64
ATT_Q_HEADS = 16
ATT_KV_HEADS = 4
ATT_GROUP = 4
WINDOW = 128
ROPE_THETA = 10000.0

FFN_DIM = 2816
FFN_CONV = 3

LANES = 128
CARRY_ROWS = 8
BF16_ROWS = 16
VMEM_LIMIT = 56 * 1024 * 1024


def _params(sem):
    return pltpu.CompilerParams(dimension_semantics=sem, vmem_limit_bytes=VMEM_LIMIT)


def _const_spec(shape):
    nd = len(shape)
    return pl.BlockSpec(shape, lambda *_: (0,) * nd, pipeline_mode=pl.Buffered(1))


CAST_BLOCKS = 16


def _cast_jobs(items, n_steps, step_index):
    inputs, in_specs, out_specs, out_shapes = [], [], [], []
    if items:
        assert n_steps % CAST_BLOCKS == 0, n_steps
    per_block = n_steps // CAST_BLOCKS
    block_of = lambda *g: step_index(*g) // per_block
    for arr, layer in items:
        r, c = arr.shape[-2:]
        rows = r // CAST_BLOCKS
        assert rows * CAST_BLOCKS == r and rows % BF16_ROWS == 0, arr.shape
        if layer is None:
            in_specs.append(pl.BlockSpec((rows, c), lambda *g: (block_of(*g), 0)))
        else:
            in_specs.append(pl.BlockSpec((None, rows, c), lambda *g, layer=layer: (layer, block_of(*g), 0)))
        out_specs.append(pl.BlockSpec((rows, c), lambda *g: (block_of(*g), 0)))
        out_shapes.append(jax.ShapeDtypeStruct((r, c), BF16))
        inputs.append(arr)
    return inputs, in_specs, out_specs, out_shapes


def _run_cast_jobs(cast_in, cast_out):
    for src, dst in zip(cast_in, cast_out):
        dst[...] = src[...].astype(BF16)


def _rms_scale(x):
    return lax.rsqrt(jnp.mean(x * x, axis=-1, keepdims=True) + EPS)


def _silu_of_half(hx):
    return hx + hx * jnp.tanh(hx)


def _causal_conv(pre, hist_ref, w_ref, b_ref, c0, width, taps):
    tm = pre.shape[0]
    cols = slice(c0, c0 + width)
    ext = jnp.concatenate([hist_ref[:, cols], pre], axis=0)
    hist_ref[:, cols] = pre[tm - CARRY_ROWS:tm, :]
    w = [w_ref[k:k + 1, cols] for k in range(taps)]
    back1 = pltpu.roll(ext, 1, axis=0)
    if taps == 4:
        near = w[3] * ext + w[2] * back1
        far = w[1] * ext + w[0] * back1
        out = near + pltpu.roll(far, 2, axis=0)
    else:
        assert taps == 3
        far = w[1] * ext + w[0] * back1
        out = w[2] * ext + pltpu.roll(far, 1, axis=0)
    return out[CARRY_ROWS:, :] + b_ref[:, cols]


def _mamba_in_kernel(*refs, chunk, n_cast):
    x_ref, g_ref, w_ref, wdt_ref, cw_ref, cb_ref, dtb_ref = refs[:7]
    cast_in = refs[7:7 + n_cast]
    z_ref, xbc_ref, dt_ref = refs[7 + n_cast:10 + n_cast]
    cast_out = refs[10 + n_cast:10 + 2 * n_cast]
    (hist_ref,) = refs[10 + 2 * n_cast:]

    @pl.when(pl.program_id(1) == 0)
    def _():
        hist_ref[...] = jnp.zeros_like(hist_ref)

    x = x_ref[...]
    h = (x * _rms_scale(x) * g_ref[...]).astype(BF16)
    z_chunks = list(range(0, SSM_INNER, chunk))
    for i, c in enumerate(range(0, SSM_XBC, chunk)):
        pre = jnp.dot(h, w_ref[:, SSM_INNER + c:SSM_INNER + c + chunk], preferred_element_type=F32)
        conv = _causal_conv(pre, hist_ref, cw_ref, cb_ref, c, chunk, SSM_CONV)
        xbc_ref[:, c:c + chunk] = _silu_of_half(conv.astype(BF16))
        if i % 2 == 1:
            zc = z_chunks[i // 2]
            z_ref[:, zc:zc + chunk] = jnp.dot(h, w_ref[:, zc:zc + chunk], preferred_element_type=F32).astype(BF16)
    dtp = jnp.dot(h, wdt_ref[...], preferred_element_type=F32) + dtb_ref[...]
    dt_ref[...] = jnp.maximum(dtp, 0.0) + jnp.log1p(jnp.exp(-jnp.abs(dtp)))
    _run_cast_jobs(cast_in, cast_out)


def _mamba_in(x2, g, w, wdt, cw, cb, dtb, *, batch, seq, tm=512, chunk=256, casts=()):
    nst = seq // tm
    row = lambda b, i: (b * nst + i, 0)
    t = batch * seq
    c_in, c_in_specs, c_out_specs, c_out_shapes = _cast_jobs(casts, batch * nst, lambda b, i: b * nst + i)
    return pl.pallas_call(
        functools.partial(_mamba_in_kernel, chunk=chunk, n_cast=len(c_in)),
        grid=(batch, nst),
        in_specs=[
            pl.BlockSpec((tm, D_MODEL), row),
            _const_spec((1, D_MODEL)),
            _const_spec((D_MODEL, SSM_PROJ)),
            _const_spec((D_MODEL, LANES)),
            _const_spec((SSM_CONV, SSM_XBC)),
            _const_spec((1, SSM_XBC)),
            _const_spec((1, LANES)),
        ] + c_in_specs,
        out_specs=[
            pl.BlockSpec((tm, SSM_INNER), row),
            pl.BlockSpec((tm, SSM_XBC), row),
            pl.BlockSpec((tm, LANES), row),
        ] + c_out_specs,
        out_shape=[
            jax.ShapeDtypeStruct((t, SSM_INNER), BF16),
            jax.ShapeDtypeStruct((t, SSM_XBC), BF16),
            jax.ShapeDtypeStruct((t, LANES), F32),
        ] + c_out_shapes,
        scratch_shapes=[pltpu.VMEM((CARRY_ROWS, SSM_XBC), F32)],
        compiler_params=_params(("arbitrary", "arbitrary")),
        name="mamba_in",
    )(x2, g, w, wdt, cw, cb, dtb, *c_in)


def _split3(a):
    hi = a.astype(BF16)
    r1 = a - hi.astype(F32)
    mid = r1.astype(BF16)
    lo = (r1 - mid.astype(F32)).astype(BF16)
    return hi, mid, lo


def _pack3(v, valid):
    hi, mid, lo = _split3(jnp.where(valid, v, 0.0))
    packed = (hi.astype(F32) + pltpu.roll(mid.astype(F32), SSM_HEADS, axis=1)
              + pltpu.roll(lo.astype(F32), 2 * SSM_HEADS, axis=1))
    return packed.astype(BF16)


def _ssd_kernel(*refs, ts, n_cast):
    xbc_ref, z_ref, dt_ref, x_ref, alog_ref, dx_ref, wo_ref = refs[:7]
    cast_in = refs[7:7 + n_cast]
    o_ref = refs[7 + n_cast]
    cast_out = refs[8 + n_cast:8 + 2 * n_cast]
    state_ref, yn_ref, sel_ref, sel64_ref = refs[8 + 2 * n_cast:]
    L = SSD_BLOCK
    GW = SSM_GROUP_WIDTH
    HPG = SSM_HEADS // SSM_GROUPS
    HD = SSM_HEAD_DIM
    b_off = SSM_INNER
    c_off = SSM_INNER + SSM_GROUPS * SSM_STATE

    @pl.when(pl.program_id(1) == 0)
    def _():
        state_ref[...] = jnp.zeros_like(state_ref)

    @pl.when((pl.program_id(0) == 0) & (pl.program_id(1) == 0))
    def _():
        for g in range(SSM_GROUPS):
            for ref, width in ((sel_ref, LANES), (sel64_ref, HD)):
                r = lax.broadcasted_iota(jnp.int32, (LANES, HPG * width), 0)
                e = lax.broadcasted_iota(jnp.int32, (LANES, HPG * width), 1) // width
                hit = (r < 3 * SSM_HEADS) & ((r % SSM_HEADS) == g * HPG + e)
                ref[g] = jnp.where(hit, 1.0, 0.0).astype(BF16)

    a_scale = -jnp.exp(alog_ref[...]) * LOG2E
    row = lax.broadcasted_iota(jnp.int32, (L, L), 0)
    col = lax.broadcasted_iota(jnp.int32, (L, L), 1)
    causal = row >= col
    tri = jnp.where(causal, 1.0, 0.0).astype(BF16)
    lane = lax.broadcasted_iota(jnp.int32, (L, LANES), 1)
    valid = lane < SSM_HEADS
    low = lane < HD
    lane_grp = lax.broadcasted_iota(jnp.int32, (L, GW), 1) // HD

    def block(c):
        r0 = pl.multiple_of(c * L, L)
        rows = pl.ds(r0, L)
        dt = dt_ref[rows, :]
        a = dt * a_scale
        acum = None
        for term in _split3(a):
            part = jnp.dot(tri, term, preferred_element_type=F32)
            acum = part if acum is None else acum + part
        acum_t = jnp.transpose(acum)
        acum_p = _pack3(acum, valid)
        dt_p = _pack3(dt, valid)

        for g0 in range(0, SSM_GROUPS, SSD_GROUP_BATCH):
            scan_groups(rows, acum_t, acum_p, dt_p, range(g0, g0 + SSD_GROUP_BATCH))

    def scan_groups(rows, acum_t, acum_p, dt_p, G):
        xs = {g: xbc_ref[rows, g * GW:(g + 1) * GW].astype(F32) for g in G}
        cg = {g: xbc_ref[rows, c_off + g * SSM_STATE:c_off + (g + 1) * SSM_STATE] for g in G}
        bgt = {g: jnp.transpose(xbc_ref[rows, b_off + g * SSM_STATE:b_off + (g + 1) * SSM_STATE]
                                .astype(F32)).astype(BF16) for g in G}
        acol = {g: jnp.dot(acum_p, sel_ref[g], preferred_element_type=F32) for g in G}
        dt_x = {g: jnp.dot(dt_p, sel64_ref[g], preferred_element_type=F32) for g in G}
        cb = {g: jnp.dot(cg[g], bgt[g], preferred_element_type=F32) for g in G}
        st = {g: state_ref[g] for g in G}
        y_in = {g: jnp.dot(cg[g], st[g].astype(BF16), preferred_element_type=F32) for g in G}

        xd, wcat, xcat, acum_x = {}, {}, {}, {}
        for g in G:
            ab = [acol[g][:, e * LANES:(e + 1) * LANES] for e in range(HPG)]
            acum_x[g] = jnp.concatenate([jnp.where(low, ab[0], ab[1]), jnp.where(low, ab[2], ab[3])], axis=1)
            xd[g] = xs[g] * dt_x[g]
            xd_b = xd[g].astype(BF16)
            ws, xparts = [], []
            for e in range(HPG):
                xparts.append(jnp.where(lane_grp == e, xd_b, jnp.zeros_like(xd_b)))
                seg = ab[e] - acum_t[g * HPG + e:g * HPG + e + 1, :]
                ws.append((cb[g] * jnp.exp2(jnp.where(causal, seg, -jnp.inf))).astype(BF16))
            wcat[g] = jnp.concatenate(ws, axis=1)
            xcat[g] = jnp.concatenate(xparts, axis=0)
        y = {g: jnp.dot(wcat[g], xcat[g], preferred_element_type=F32) for g in G}

        to_end, alast = {}, {}
        for g in G:
            alast[g] = acum_x[g][L - 1:L, :]
            alast_x = alast[g]
            y[g] = y[g] + y_in[g] * jnp.exp2(acum_x[g])
            to_end[g] = (xd[g] * jnp.exp2(alast_x - acum_x[g])).astype(BF16)
        for g in G:
            alast_x = alast[g]
            state_ref[g] = st[g] * jnp.exp2(alast_x) + jnp.dot(bgt[g], to_end[g], preferred_element_type=F32)
        for g in G:
            yg = y[g] + xs[g] * dx_ref[:, g * GW:(g + 1) * GW]
            yg = yg * _silu_of_half(z_ref[rows, g * GW:(g + 1) * GW].astype(F32))
            yn = yg * _rms_scale(yg)
            yn_ref[rows, g * GW:(g + 1) * GW] = yn.astype(BF16)

    def step(c, carry):
        block(c)
        return carry

    lax.fori_loop(0, ts // L, step, 0)
    o_ref[...] = x_ref[...] + jnp.dot(yn_ref[...], wo_ref[...], preferred_element_type=F32)
    _run_cast_jobs(cast_in, cast_out)


def _ssd(xbc, z, dt, x2, alog, dx, wo, *, batch, seq, ts=512, casts=()):
    nst = seq // ts
    row = lambda b, i: (b * nst + i, 0)
    t = batch * seq
    c_in, c_in_specs, c_out_specs, c_out_shapes = _cast_jobs(casts, batch * nst, lambda b, i: b * nst + i)
    return pl.pallas_call(
        functools.partial(_ssd_kernel, ts=ts, n_cast=len(c_in)),
        grid=(batch, nst),
        in_specs=[
            pl.BlockSpec((ts, SSM_XBC), row),
            pl.BlockSpec((ts, SSM_INNER), row),
            pl.BlockSpec((ts, LANES), row),
            pl.BlockSpec((ts, D_MODEL), row),
            _const_spec((1, LANES)),
            _const_spec((1, SSM_INNER)),
            _const_spec((SSM_INNER, D_MODEL)),
        ] + c_in_specs,
        out_specs=[pl.BlockSpec((ts, D_MODEL), row)] + c_out_specs,
        out_shape=[jax.ShapeDtypeStruct((t, D_MODEL), F32)] + c_out_shapes,
        scratch_shapes=[
            pltpu.VMEM((SSM_GROUPS, SSM_STATE, SSM_GROUP_WIDTH), F32),
            pltpu.VMEM((ts, SSM_INNER), BF16),
            pltpu.VMEM((SSM_GROUPS, LANES, (SSM_HEADS // SSM_GROUPS) * LANES), BF16),
            pltpu.VMEM((SSM_GROUPS, LANES, SSM_GROUP_WIDTH), BF16),
        ],
        compiler_params=_params(("arbitrary", "arbitrary")),
        name="ssd_out",
    )(xbc, z, dt, x2, alog, dx, wo, *c_in)


_FFN_CHUNKS = ((0, 512), (512, 512), (1024, 512), (1536, 512), (2048, 512), (2560, 256))


def _ffn_kernel(*refs, n_cast):
    x_ref, g_ref, win_ref, cw_ref, cb_ref, wd_ref = refs[:6]
    cast_in = refs[6:6 + n_cast]
    o_ref = refs[6 + n_cast]
    cast_out = refs[7 + n_cast:7 + 2 * n_cast]
    hist_ref, hid_ref = refs[7 + 2 * n_cast:]

    @pl.when(pl.program_id(1) == 0)
    def _():
        hist_ref[...] = jnp.zeros_like(hist_ref)

    x = x_ref[...]
    h = (x * _rms_scale(x) * g_ref[...]).astype(BF16)
    for c, width in _FFN_CHUNKS:
        gate = jnp.dot(h, win_ref[:, c:c + width], preferred_element_type=F32)
        val = jnp.dot(h, win_ref[:, FFN_DIM + c:FFN_DIM + c + width], preferred_element_type=F32)
        conv = _causal_conv(gate, hist_ref, cw_ref, cb_ref, c, width, FFN_CONV)
        hid_ref[:, c:c + width] = (_silu_of_half(conv) * val).astype(BF16)
    o_ref[...] = x + jnp.dot(hid_ref[...], wd_ref[...], preferred_element_type=F32)
    _run_cast_jobs(cast_in, cast_out)


def _ffn(x2, g, win, cw, cb, wd, *, batch, seq, tm=1024, casts=()):
    nst = seq // tm
    row = lambda b, i: (b * nst + i, 0)
    t = batch * seq
    c_in, c_in_specs, c_out_specs, c_out_shapes = _cast_jobs(casts, batch * nst, lambda b, i: b * nst + i)
    return pl.pallas_call(
        functools.partial(_ffn_kernel, n_cast=len(c_in)),
        grid=(batch, nst),
        in_specs=[
            pl.BlockSpec((tm, D_MODEL), row),
            _const_spec((1, D_MODEL)),
            _const_spec((D_MODEL, 2 * FFN_DIM)),
            _const_spec((FFN_CONV, FFN_DIM)),
            _const_spec((1, FFN_DIM)),
            _const_spec((FFN_DIM, D_MODEL)),
        ] + c_in_specs,
        out_specs=[pl.BlockSpec((tm, D_MODEL), row)] + c_out_specs,
        out_shape=[jax.ShapeDtypeStruct((t, D_MODEL), F32)] + c_out_shapes,
        scratch_shapes=[
            pltpu.VMEM((CARRY_ROWS, FFN_DIM), F32),
            pltpu.VMEM((tm, FFN_DIM), BF16),
        ],
        compiler_params=_params(("arbitrary", "arbitrary")),
        name="conv_ffn",
    )(x2, g, win, cw, cb, wd, *c_in)


def _rope_kernel(pos_ref, invf_ref, cos_ref, sin_ref, *, tp):
    per_row = LANES // (ATT_HEAD_DIM // 2)
    ang = pos_ref[...].astype(F32) * invf_ref[...]
    quarter = lax.broadcasted_iota(jnp.int32, (tp, LANES), 1) // (ATT_HEAD_DIM // 2)
    for arr, out_ref in ((jnp.cos(ang), cos_ref), (jnp.sin(ang), sin_ref)):
        rolled = [arr] + [pltpu.roll(arr, m * (ATT_HEAD_DIM // 2), axis=1) for m in range(1, per_row)]
        for j in range(per_row):
            t = rolled[(0 - j) % per_row]
            for k in range(1, per_row):
                t = jnp.where(quarter == k, rolled[(k - j) % per_row], t)
            out_ref[pl.ds(j, tp, stride=per_row), :] = t


def _rope_tables(pos_packed, invf, *, tp=512):
    rows = pos_packed.shape[0]
    tp = min(tp, rows)
    per_row = LANES // (ATT_HEAD_DIM // 2)
    return pl.pallas_call(
        functools.partial(_rope_kernel, tp=tp),
        grid=(rows // tp,),
        in_specs=[pl.BlockSpec((tp, LANES), lambda i: (i, 0)), _const_spec((1, LANES))],
        out_specs=[pl.BlockSpec((per_row * tp, LANES), lambda i: (i, 0))] * 2,
        out_shape=[jax.ShapeDtypeStruct((per_row * rows, LANES), F32)] * 2,
        compiler_params=_params(("arbitrary",)),
        name="rope_tables",
    )(pos_packed, invf)


def _head_norm_rope(y, gain_ref, cos, sin_signed, first_half, ones_bd):
    outs = []
    w = y.shape[1]
    for c in range(0, w, 2 * LANES):
        blk = y[:, c:c + 2 * LANES]
        ss = jnp.dot((blk * blk).astype(BF16), ones_bd, preferred_element_type=F32)
        yn = blk * lax.rsqrt(ss * (1.0 / ATT_HEAD_DIM) + EPS) * gain_ref[:, c:c + 2 * LANES]
        for j in range(0, 2 * LANES, LANES):
            v = yn[:, j:j + LANES]
            lo = pltpu.roll(v, ATT_HEAD_DIM // 2, axis=1)
            hi = pltpu.roll(v, LANES - ATT_HEAD_DIM // 2, axis=1)
            rot = jnp.where(first_half, hi, lo)
            outs.append(v * cos + rot * sin_signed)
    return outs


def _dup_heads(blocks, low):
    outs = []
    for v in blocks:
        swapped = pltpu.roll(v, ATT_HEAD_DIM, axis=1)
        outs.append(jnp.where(low, v, swapped).astype(BF16))
        outs.append(jnp.where(low, swapped, v).astype(BF16))
    return jnp.concatenate(outs, axis=1)


def _qkv_kernel(x_ref, cos_ref, sin_ref, gkv_ref, gq_ref, wkv_ref, bkv_ref, wq_ref, bq_ref,
                kn_ref, qn_ref, q_ref, k_ref, v_ref, *, tm):
    x = x_ref[...]
    xn = x * _rms_scale(x)
    hkv = (xn * gkv_ref[...]).astype(BF16)
    hq = (xn * gq_ref[...]).astype(BF16)
    kw = ATT_KV_HEADS * ATT_HEAD_DIM

    lane = lax.broadcasted_iota(jnp.int32, (tm, LANES), 1)
    first_half = (lane % ATT_HEAD_DIM) < (ATT_HEAD_DIM // 2)
    low = lane < ATT_HEAD_DIM
    cos = cos_ref[...]
    sin_signed = jnp.where(first_half, -sin_ref[...], sin_ref[...])
    r = lax.broadcasted_iota(jnp.int32, (2 * LANES, 2 * LANES), 0) // ATT_HEAD_DIM
    c = lax.broadcasted_iota(jnp.int32, (2 * LANES, 2 * LANES), 1) // ATT_HEAD_DIM
    ones_bd = jnp.where(r == c, 1.0, 0.0).astype(BF16)

    kv = jnp.dot(hkv, wkv_ref[...], preferred_element_type=F32) + bkv_ref[...]
    k_ref[...] = _dup_heads(_head_norm_rope(kv[:, :kw], kn_ref, cos, sin_signed, first_half, ones_bd), low)
    v_ref[...] = _dup_heads([kv[:, kw + j:kw + j + LANES] for j in range(0, kw, LANES)], low)
    q = jnp.dot(hq, wq_ref[...], preferred_element_type=F32) + bq_ref[...]
    qs = _head_norm_rope(q, qn_ref, cos, sin_signed, first_half, ones_bd)
    q_ref[...] = jnp.concatenate([v.astype(BF16) for v in qs], axis=1)


def _qkv(x2, cos, sin, gkv, gq, wkv, bkv, wq, bq, kn, qn, *, tm=1024):
    t = x2.shape[0]
    kw = ATT_KV_HEADS * ATT_HEAD_DIM
    qw = ATT_Q_HEADS * ATT_HEAD_DIM
    row = lambda i: (i, 0)
    return pl.pallas_call(
        functools.partial(_qkv_kernel, tm=tm),
        grid=(t // tm,),
        in_specs=[
            pl.BlockSpec((tm, D_MODEL), row),
            pl.BlockSpec((tm, LANES), row),
            pl.BlockSpec((tm, LANES), row),
            _const_spec((1, D_MODEL)),
            _const_spec((1, D_MODEL)),
            _const_spec((D_MODEL, 2 * kw)),
            _const_spec((1, 2 * kw)),
            _const_spec((D_MODEL, qw)),
            _const_spec((1, qw)),
            _const_spec((1, kw)),
            _const_spec((1, qw)),
        ],
        out_specs=[
            pl.BlockSpec((tm, qw), row),
            pl.BlockSpec((tm, 2 * kw), row),
            pl.BlockSpec((tm, 2 * kw), row),
        ],
        out_shape=[
            jax.ShapeDtypeStruct((t, qw), BF16),
            jax.ShapeDtypeStruct((t, 2 * kw), BF16),
            jax.ShapeDtypeStruct((t, 2 * kw), BF16),
        ],
        compiler_params=_params(("arbitrary",)),
        name="qkv_rope",
    )(x2, cos, sin, gkv, gq, wkv, bkv, wq, bq, kn, qn)


def _attn_kernel(sink_ref, q_ref, k_ref, v_ref, x_ref, wo_ref, bo_ref, o_ref, att_ref, *, tq):
    i = pl.program_id(1)
    W = WINDOW
    G = ATT_GROUP
    low = lax.broadcasted_iota(jnp.int32, (W, LANES), 1) < ATT_HEAD_DIM
    qi = lax.broadcasted_iota(jnp.int32, (W, 2 * W), 0)
    ki = lax.broadcasted_iota(jnp.int32, (W, 2 * W), 1)
    ones = jnp.ones((2 * W, LANES), BF16)

    for jb in range(tq // W):
        q0 = i * tq + jb * W
        start = pl.multiple_of(jnp.maximum(q0 - W, 0), W)
        rel = qi + (q0 - start) - ki
        neg = jnp.where((rel >= 0) & (rel < W), 0.0, -jnp.inf)
        rows = slice(jb * W, (jb + 1) * W)
        for hk in range(ATT_KV_HEADS):
            kh = k_ref[pl.ds(start, 2 * W), hk * LANES:(hk + 1) * LANES]
            vh = v_ref[pl.ds(start, 2 * W), hk * LANES:(hk + 1) * LANES]
            parts = []
            for g in range(G):
                qb = q_ref[rows, (hk * G + g) // 2 * LANES:((hk * G + g) // 2 + 1) * LANES]
                keep = low if g % 2 == 0 else jnp.logical_not(low)
                parts.append(jnp.where(keep, qb, jnp.zeros_like(qb)))
            qst = jnp.concatenate(parts, axis=0)
            s = lax.dot_general(qst, kh, (((1,), (1,)), ((), ())), preferred_element_type=F32)
            ps, ms = [], []
            for g in range(G):
                sg = s[g * W:(g + 1) * W, :] + neg
                m = jnp.maximum(jnp.max(sg, axis=-1, keepdims=True), sink_ref[hk * G + g])
                ps.append(jnp.exp(sg - m).astype(BF16))
                ms.append(m)
            pv = jnp.dot(jnp.concatenate(ps, axis=0), jnp.concatenate([vh, ones], axis=1),
                         preferred_element_type=F32)
            outs = []
            for g in range(G):
                blk = pv[g * W:(g + 1) * W, :]
                denom = blk[:, LANES:] + jnp.exp(sink_ref[hk * G + g] - ms[g])
                outs.append(blk[:, :LANES] / denom)
            for half in range(G // 2):
                col = (hk * G) // 2 + half
                att_ref[rows, col * LANES:(col + 1) * LANES] = jnp.where(
                    low, outs[2 * half], outs[2 * half + 1]).astype(BF16)
    o_ref[...] = (x_ref[...] + bo_ref[...]
                  + jnp.dot(att_ref[...], wo_ref[...], preferred_element_type=F32))


def _attn(sinks, q, k, v, x2, wo, bo, *, batch, seq, tq=512):
    nst = seq // tq
    qw = ATT_Q_HEADS * ATT_HEAD_DIM
    kw = ATT_KV_HEADS * ATT_HEAD_DIM
    row = lambda b, i: (b * nst + i, 0)
    per_batch = lambda b, i: (b, 0)
    t = batch * seq
    return pl.pallas_call(
        functools.partial(_attn_kernel, tq=tq),
        grid=(batch, nst),
        in_specs=[
            pl.BlockSpec(memory_space=pltpu.SMEM),
            pl.BlockSpec((tq, qw), row),
            pl.BlockSpec((seq, 2 * kw), per_batch),
            pl.BlockSpec((seq, 2 * kw), per_batch),
            pl.BlockSpec((tq, D_MODEL), row),
            _const_spec((qw, D_MODEL)),
            _const_spec((1, D_MODEL)),
        ],
        out_specs=pl.BlockSpec((tq, D_MODEL), row),
        out_shape=jax.ShapeDtypeStruct((t, D_MODEL), F32),
        scratch_shapes=[pltpu.VMEM((tq, qw), BF16)],
        compiler_params=_params(("arbitrary", "arbitrary")),
        name="swa_attn",
    )(sinks, q, k, v, x2, wo, bo)


def _row(v, width=None):
    v = v.reshape(1, -1).astype(F32)
    if width is not None and v.shape[1] < width:
        v = jnp.pad(v, ((0, 0), (0, width - v.shape[1])))
    return v


def kernel(x, positions, a_norm, a_in_proj, a_conv_w, a_conv_b, a_dt_bias, a_A_log, a_D, a_gnorm,
           a_out_proj, kv_norm, w_kv, b_kv, k_norm, b_norm, w_q, b_q, q_norm, sinks, w_o, b_o,
           f_norm, f_w_in, f_conv_w, f_conv_b, f_w_down):
    batch, seq, _ = x.shape
    t = batch * seq
    x2 = x.reshape(t, D_MODEL)

    w_in = a_in_proj.reshape(D_MODEL, SSM_PROJ)
    w_dt = jnp.pad(w_in[:, SSM_INNER + SSM_XBC:], ((0, 0), (0, LANES - SSM_HEADS))).astype(BF16)
    col_scale = jnp.where(jnp.arange(SSM_PROJ) < SSM_INNER, 0.5, 1.0).astype(F32)
    w_main = (w_in * col_scale).astype(BF16)
    z, xbc, dt, win0, wd0 = _mamba_in(
        x2, _row(a_norm[0]), w_main, w_dt, 0.5 * a_conv_w[0].astype(F32), 0.5 * _row(a_conv_b[0]),
        _row(a_dt_bias[0], LANES), batch=batch, seq=seq, casts=[(f_w_in, 0), (f_w_down, 0)])
    dx = jnp.repeat(a_D[0].astype(F32), SSM_HEAD_DIM).reshape(1, SSM_INNER)
    wo_ssm = (a_gnorm[0].astype(F32)[:, None] * a_out_proj.reshape(SSM_INNER, D_MODEL)).astype(BF16)
    later = [(f_w_in, 1), (f_w_down, 1), (w_kv, None), (w_q.reshape(D_MODEL, -1), None),
             (w_o.reshape(-1, D_MODEL), None)]
    x2, win1, wd1, wkv_b, wq_b, wo_b = _ssd(xbc, z, dt, x2, _row(a_A_log[0], LANES), dx, wo_ssm,
                                            batch=batch, seq=seq, casts=later)
    f_cw = 0.5 * f_conv_w.astype(F32)
    f_cb = 0.5 * f_conv_b.astype(F32)
    (x2,) = _ffn(x2, _row(f_norm[0]), win0, f_cw[0], _row(f_cb[0]), wd0, batch=batch, seq=seq)

    half = ATT_HEAD_DIM // 2
    inv_freq = ROPE_THETA ** (-jnp.arange(half, dtype=F32) / half)
    invf = jnp.tile(inv_freq, LANES // half).reshape(1, LANES)
    per_row = LANES // half
    pos_packed = jnp.repeat(positions.reshape(t // per_row, per_row), half, axis=1)
    cos, sin = _rope_tables(pos_packed, invf)
    q_gain = jnp.tile(q_norm[0].astype(F32), ATT_Q_HEADS) * (ATT_HEAD_DIM ** -0.5)
    q, k, v = _qkv(x2, cos, sin, _row(kv_norm), _row(b_norm[0]), wkv_b, _row(b_kv), wq_b, _row(b_q[0]),
                   _row(jnp.tile(k_norm, ATT_KV_HEADS)), _row(q_gain))
    x2 = _attn(sinks[0].astype(F32), q, k, v, x2, wo_b, _row(b_o[0]), batch=batch, seq=seq)
    (x2,) = _ffn(x2, _row(f_norm[1]), win1, f_cw[1], _row(f_cb[1]), wd1, batch=batch, seq=seq)
    return x2.reshape(batch, seq, D_MODEL)
```

```python
import functools

import jax
import jax.numpy as jnp
from jax import lax
from jax.experimental import pallas as pl
from jax.experimental.pallas import tpu as pltpu

F32 = jnp.float32
BF16 = jnp.bfloat16

D_MODEL = 1024
EPS = 1e-5

SSM_INNER = 2048
SSM_HEAD_DIM = 64
SSM_HEADS = 32
SSM_GROUPS = 8
SSM_STATE = 128
SSM_CONV = 4
SSD_BLOCK = 128
LOG2E = 1.4426950408889634
SSD_GROUP_BATCH = 2
SSM_XBC = SSM_INNER + 2 * SSM_GROUPS * SSM_STATE
SSM_GROUP_WIDTH = SSM_INNER // SSM_GROUPS
SSM_PROJ = SSM_INNER + SSM_XBC + SSM_HEADS

ATT_HEAD_DIM = 64
ATT_Q_HEADS = 16
ATT_KV_HEADS = 4
ATT_GROUP = 4
WINDOW = 128
ROPE_THETA = 10000.0

FFN_DIM = 2816
FFN_CONV = 3

LANES = 128
CARRY_ROWS = 8
BF16_ROWS = 16
VMEM_LIMIT = 60 * 1024 * 1024


def _params(sem):
    return pltpu.CompilerParams(dimension_semantics=sem, vmem_limit_bytes=VMEM_LIMIT)


def _const_spec(shape):
    nd = len(shape)
    return pl.BlockSpec(shape, lambda *_: (0,) * nd, pipeline_mode=pl.Buffered(1))


CAST_BLOCKS = 16


def _cast_jobs(items, n_steps, step_index):
    inputs, in_specs, out_specs, out_shapes = [], [], [], []
    if items:
        assert n_steps % CAST_BLOCKS == 0, n_steps
    per_block = n_steps // CAST_BLOCKS
    block_of = lambda *g: step_index(*g) // per_block
    for arr, layer in items:
        r, c = arr.shape[-2:]
        rows = r // CAST_BLOCKS
        assert rows * CAST_BLOCKS == r and rows % BF16_ROWS == 0, arr.shape
        if layer is None:
            in_specs.append(pl.BlockSpec((rows, c), lambda *g: (block_of(*g), 0)))
        else:
            in_specs.append(pl.BlockSpec((None, rows, c), lambda *g, layer=layer: (layer, block_of(*g), 0)))
        out_specs.append(pl.BlockSpec((rows, c), lambda *g: (block_of(*g), 0)))
        out_shapes.append(jax.ShapeDtypeStruct((r, c), BF16))
        inputs.append(arr)
    return inputs, in_specs, out_specs, out_shapes


def _run_cast_jobs(cast_in, cast_out):
    for src, dst in zip(cast_in, cast_out):
        dst[...] = src[...].astype(BF16)


def _rms_scale(x):
    return lax.rsqrt(jnp.mean(x * x, axis=-1, keepdims=True) + EPS)


def _silu_of_half(hx):
    return hx + hx * jnp.tanh(hx)


def _causal_conv(pre, hist_ref, w_ref, b_ref, c0, width, taps):
    tm = pre.shape[0]
    cols = slice(c0, c0 + width)
    ext = jnp.concatenate([hist_ref[:, cols], pre], axis=0)
    hist_ref[:, cols] = pre[tm - CARRY_ROWS:tm, :]
    w = [w_ref[k:k + 1, cols] for k in range(taps)]
    back1 = pltpu.roll(ext, 1, axis=0)
    if taps == 4:
        near = w[3] * ext + w[2] * back1
        far = w[1] * ext + w[0] * back1
        out = near + pltpu.roll(far, 2, axis=0)
    else:
        assert taps == 3
        far = w[1] * ext + w[0] * back1
        out = w[2] * ext + pltpu.roll(far, 1, axis=0)
    return out[CARRY_ROWS:, :] + b_ref[:, cols]


def _mamba_in_kernel(*refs, chunk, n_cast):
    x_ref, g_ref, w_ref, wdt_ref, cw_ref, cb_ref, dtb_ref = refs[:7]
    cast_in = refs[7:7 + n_cast]
    z_ref, xbc_ref, dt_ref = refs[7 + n_cast:10 + n_cast]
    cast_out = refs[10 + n_cast:10 + 2 * n_cast]
    (hist_ref,) = refs[10 + 2 * n_cast:]

    @pl.when(pl.program_id(1) == 0)
    def _():
        hist_ref[...] = jnp.zeros_like(hist_ref)

    x = x_ref[...]
    h = (x * _rms_scale(x) * g_ref[...]).astype(BF16)
    z_chunks = list(range(0, SSM_INNER, chunk))
    for i, c in enumerate(range(0, SSM_XBC, chunk)):
        pre = jnp.dot(h, w_ref[:, SSM_INNER + c:SSM_INNER + c + chunk], preferred_element_type=F32)
        conv = _causal_conv(pre, hist_ref, cw_ref, cb_ref, c, chunk, SSM_CONV)
        xbc_ref[:, c:c + chunk] = _silu_of_half(conv.astype(BF16))
        if i % 2 == 1:
            zc = z_chunks[i // 2]
            z_ref[:, zc:zc + chunk] = jnp.dot(h, w_ref[:, zc:zc + chunk], preferred_element_type=F32).astype(BF16)
    dtp = jnp.dot(h, wdt_ref[...], preferred_element_type=F32) + dtb_ref[...]
    dt_ref[...] = jnp.maximum(dtp, 0.0) + jnp.log1p(jnp.exp(-jnp.abs(dtp)))
    _run_cast_jobs(cast_in, cast_out)


def _mamba_in(x2, g, w, wdt, cw, cb, dtb, *, batch, seq, tm=1024, chunk=256, casts=()):
    nst = seq // tm
    row = lambda b, i: (b * nst + i, 0)
    t = batch * seq
    c_in, c_in_specs, c_out_specs, c_out_shapes = _cast_jobs(casts, batch * nst, lambda b, i: b * nst + i)
    return pl.pallas_call(
        functools.partial(_mamba_in_kernel, chunk=chunk, n_cast=len(c_in)),
        grid=(batch, nst),
        in_specs=[
            pl.BlockSpec((tm, D_MODEL), row),
            _const_spec((1, D_MODEL)),
            _const_spec((D_MODEL, SSM_PROJ)),
            _const_spec((D_MODEL, LANES)),
            _const_spec((SSM_CONV, SSM_XBC)),
            _const_spec((1, SSM_XBC)),
            _const_spec((1, LANES)),
        ] + c_in_specs,
        out_specs=[
            pl.BlockSpec((tm, SSM_INNER), row),
            pl.BlockSpec((tm, SSM_XBC), row),
            pl.BlockSpec((tm, LANES), row),
        ] + c_out_specs,
        out_shape=[
            jax.ShapeDtypeStruct((t, SSM_INNER), BF16),
            jax.ShapeDtypeStruct((t, SSM_XBC), BF16),
            jax.ShapeDtypeStruct((t, LANES), F32),
        ] + c_out_shapes,
        scratch_shapes=[pltpu.VMEM((CARRY_ROWS, SSM_XBC), F32)],
        compiler_params=_params(("arbitrary", "arbitrary")),
        name="mamba_in",
    )(x2, g, w, wdt, cw, cb, dtb, *c_in)


def _split3(a):
    hi = a.astype(BF16)
    r1 = a - hi.astype(F32)
    mid = r1.astype(BF16)
    lo = (r1 - mid.astype(F32)).astype(BF16)
    return hi, mid, lo


def _pack3(v, valid):
    hi, mid, lo = _split3(jnp.where(valid, v, 0.0))
    packed = (hi.astype(F32) + pltpu.roll(mid.astype(F32), SSM_HEADS, axis=1)
              + pltpu.roll(lo.astype(F32), 2 * SSM_HEADS, axis=1))
    return packed.astype(BF16)


def _ssd_kernel(*refs, ts, n_cast):
    xbc_ref, z_ref, dt_ref, x_ref, alog_ref, dx_ref, wo_ref = refs[:7]
    cast_in = refs[7:7 + n_cast]
    o_ref = refs[7 + n_cast]
    cast_out = refs[8 + n_cast:8 + 2 * n_cast]
    state_ref, yn_ref, sel_ref, sel64_ref = refs[8 + 2 * n_cast:]
    L = SSD_BLOCK
    GW = SSM_GROUP_WIDTH
    HPG = SSM_HEADS // SSM_GROUPS
    HD = SSM_HEAD_DIM
    b_off = SSM_INNER
    c_off = SSM_INNER + SSM_GROUPS * SSM_STATE

    @pl.when(pl.program_id(1) == 0)
    def _():
        state_ref[...] = jnp.zeros_like(state_ref)

    @pl.when((pl.program_id(0) == 0) & (pl.program_id(1) == 0))
    def _():
        for g in range(SSM_GROUPS):
            for ref, width in ((sel_ref, LANES), (sel64_ref, HD)):
                r = lax.broadcasted_iota(jnp.int32, (LANES, HPG * width), 0)
                e = lax.broadcasted_iota(jnp.int32, (LANES, HPG * width), 1) // width
                hit = (r < 3 * SSM_HEADS) & ((r % SSM_HEADS) == g * HPG + e)
                ref[g] = jnp.where(hit, 1.0, 0.0).astype(BF16)

    a_scale = -jnp.exp(alog_ref[...]) * LOG2E
    row = lax.broadcasted_iota(jnp.int32, (L, L), 0)
    col = lax.broadcasted_iota(jnp.int32, (L, L), 1)
    causal = row >= col
    tri = jnp.where(causal, 1.0, 0.0).astype(BF16)
    lane = lax.broadcasted_iota(jnp.int32, (L, LANES), 1)
    valid = lane < SSM_HEADS
    low = lane < HD
    lane_grp = lax.broadcasted_iota(jnp.int32, (L, GW), 1) // HD

    def block(c):
        r0 = pl.multiple_of(c * L, L)
        rows = pl.ds(r0, L)
        dt = dt_ref[rows, :]
        a = dt * a_scale
        acum = None
        for term in _split3(a):
            part = jnp.dot(tri, term, preferred_element_type=F32)
            acum = part if acum is None else acum + part
        acum_t = jnp.transpose(acum)
        acum_p = _pack3(acum, valid)
        dt_p = _pack3(dt, valid)

        for g0 in range(0, SSM_GROUPS, SSD_GROUP_BATCH):
            scan_groups(rows, acum_t, acum_p, dt_p, range(g0, g0 + SSD_GROUP_BATCH))

    def scan_groups(rows, acum_t, acum_p, dt_p, G):
        xs = {g: xbc_ref[rows, g * GW:(g + 1) * GW].astype(F32) for g in G}
        cg = {g: xbc_ref[rows, c_off + g * SSM_STATE:c_off + (g + 1) * SSM_STATE] for g in G}
        bgt = {g: jnp.transpose(xbc_ref[rows, b_off + g * SSM_STATE:b_off + (g + 1) * SSM_STATE]
                                .astype(F32)).astype(BF16) for g in G}
        acol = {g: jnp.dot(acum_p, sel_ref[g], preferred_element_type=F32) for g in G}
        dt_x = {g: jnp.dot(dt_p, sel64_ref[g], preferred_element_type=F32) for g in G}
        cb = {g: jnp.dot(cg[g], bgt[g], preferred_element_type=F32) for g in G}
        st = {g: state_ref[g] for g in G}
        y_in = {g: jnp.dot(cg[g], st[g].astype(BF16), preferred_element_type=F32) for g in G}

        xd, wcat, xcat, acum_x = {}, {}, {}, {}
        for g in G:
            ab = [acol[g][:, e * LANES:(e + 1) * LANES] for e in range(HPG)]
            acum_x[g] = jnp.concatenate([jnp.where(low, ab[0], ab[1]), jnp.where(low, ab[2], ab[3])], axis=1)
            xd[g] = xs[g] * dt_x[g]
            xd_b = xd[g].astype(BF16)
            ws, xparts = [], []
            for e in range(HPG):
                xparts.append(jnp.where(lane_grp == e, xd_b, jnp.zeros_like(xd_b)))
                seg = ab[e] - acum_t[g * HPG + e:g * HPG + e + 1, :]
                ws.append((cb[g] * jnp.exp2(jnp.where(causal, seg, -jnp.inf))).astype(BF16))
            wcat[g] = jnp.concatenate(ws, axis=1)
            xcat[g] = jnp.concatenate(xparts, axis=0)
        y = {g: jnp.dot(wcat[g], xcat[g], preferred_element_type=F32) for g in G}

        to_end, alast = {}, {}
        for g in G:
            alast[g] = acum_x[g][L - 1:L, :]
            alast_x = alast[g]
            y[g] = y[g] + y_in[g] * jnp.exp2(acum_x[g])
            to_end[g] = (xd[g] * jnp.exp2(alast_x - acum_x[g])).astype(BF16)
        for g in G:
            alast_x = alast[g]
            state_ref[g] = st[g] * jnp.exp2(alast_x) + jnp.dot(bgt[g], to_end[g], preferred_element_type=F32)
        for g in G:
            yg = y[g] + xs[g] * dx_ref[:, g * GW:(g + 1) * GW]
            yg = yg * _silu_of_half(0.5 * z_ref[rows, g * GW:(g + 1) * GW].astype(F32))
            yn = yg * _rms_scale(yg)
            yn_ref[rows, g * GW:(g + 1) * GW] = yn.astype(BF16)

    def step(c, carry):
        block(c)
        return carry

    lax.fori_loop(0, ts // L, step, 0)
    o_ref[...] = x_ref[...] + jnp.dot(yn_ref[...], wo_ref[...], preferred_element_type=F32)
    _run_cast_jobs(cast_in, cast_out)


def _ssd(xbc, z, dt, x2, alog, dx, wo, *, batch, seq, ts=512, casts=()):
    nst = seq // ts
    row = lambda b, i: (b * nst + i, 0)
    t = batch * seq
    c_in, c_in_specs, c_out_specs, c_out_shapes = _cast_jobs(casts, batch * nst, lambda b, i: b * nst + i)
    return pl.pallas_call(
        functools.partial(_ssd_kernel, ts=ts, n_cast=len(c_in)),
        grid=(batch, nst),
        in_specs=[
            pl.BlockSpec((ts, SSM_XBC), row),
            pl.BlockSpec((ts, SSM_INNER), row),
            pl.BlockSpec((ts, LANES), row),
            pl.BlockSpec((ts, D_MODEL), row),
            _const_spec((1, LANES)),
            _const_spec((1, SSM_INNER)),
            _const_spec((SSM_INNER, D_MODEL)),
        ] + c_in_specs,
        out_specs=[pl.BlockSpec((ts, D_MODEL), row)] + c_out_specs,
        out_shape=[jax.ShapeDtypeStruct((t, D_MODEL), F32)] + c_out_shapes,
        scratch_shapes=[
            pltpu.VMEM((SSM_GROUPS, SSM_STATE, SSM_GROUP_WIDTH), F32),
            pltpu.VMEM((ts, SSM_INNER), BF16),
            pltpu.VMEM((SSM_GROUPS, LANES, (SSM_HEADS // SSM_GROUPS) * LANES), BF16),
            pltpu.VMEM((SSM_GROUPS, LANES, SSM_GROUP_WIDTH), BF16),
        ],
        compiler_params=_params(("arbitrary", "arbitrary")),
        name="ssd_out",
    )(xbc, z, dt, x2, alog, dx, wo, *c_in)


_FFN_CHUNKS = ((0, 512), (512, 512), (1024, 512), (1536, 512), (2048, 512), (2560, 256))


def _ffn_kernel(*refs, n_cast):
    x_ref, g_ref, win_ref, cw_ref, cb_ref, wd_ref = refs[:6]
    cast_in = refs[6:6 + n_cast]
    o_ref = refs[6 + n_cast]
    cast_out = refs[7 + n_cast:7 + 2 * n_cast]
    hist_ref, hid_ref = refs[7 + 2 * n_cast:]

    @pl.when(pl.program_id(1) == 0)
    def _():
        hist_ref[...] = jnp.zeros_like(hist_ref)

    x = x_ref[...]
    h = (x * _rms_scale(x) * g_ref[...]).astype(BF16)
    for c, width in _FFN_CHUNKS:
        gate = jnp.dot(h, win_ref[:, c:c + width], preferred_element_type=F32)
        val = jnp.dot(h, win_ref[:, FFN_DIM + c:FFN_DIM + c + width], preferred_element_type=F32)
        conv = _causal_conv(gate, hist_ref, cw_ref, cb_ref, c, width, FFN_CONV)
        hid_ref[:, c:c + width] = (_silu_of_half(conv) * val).astype(BF16)
    o_ref[...] = x + jnp.dot(hid_ref[...], wd_ref[...], preferred_element_type=F32)
    _run_cast_jobs(cast_in, cast_out)


def _ffn(x2, g, win, cw, cb, wd, *, batch, seq, tm=1024, casts=()):
    nst = seq // tm
    row = lambda b, i: (b * nst + i, 0)
    t = batch * seq
    c_in, c_in_specs, c_out_specs, c_out_shapes = _cast_jobs(casts, batch * nst, lambda b, i: b * nst + i)
    return pl.pallas_call(
        functools.partial(_ffn_kernel, n_cast=len(c_in)),
        grid=(batch, nst),
        in_specs=[
            pl.BlockSpec((tm, D_MODEL), row),
            _const_spec((1, D_MODEL)),
            _const_spec((D_MODEL, 2 * FFN_DIM)),
            _const_spec((FFN_CONV, FFN_DIM)),
            _const_spec((1, FFN_DIM)),
            _const_spec((FFN_DIM, D_MODEL)),
        ] + c_in_specs,
        out_specs=[pl.BlockSpec((tm, D_MODEL), row)] + c_out_specs,
        out_shape=[jax.ShapeDtypeStruct((t, D_MODEL), F32)] + c_out_shapes,
        scratch_shapes=[
            pltpu.VMEM((CARRY_ROWS, FFN_DIM), F32),
            pltpu.VMEM((tm, FFN_DIM), BF16),
        ],
        compiler_params=_params(("arbitrary", "arbitrary")),
        name="conv_ffn",
    )(x2, g, win, cw, cb, wd, *c_in)


def _rope_kernel(pos_ref, invf_ref, cos_ref, sin_ref, *, tp):
    per_row = LANES // (ATT_HEAD_DIM // 2)
    ang = pos_ref[...].astype(F32) * invf_ref[...]
    quarter = lax.broadcasted_iota(jnp.int32, (tp, LANES), 1) // (ATT_HEAD_DIM // 2)
    for arr, out_ref in ((jnp.cos(ang), cos_ref), (jnp.sin(ang), sin_ref)):
        rolled = [arr] + [pltpu.roll(arr, m * (ATT_HEAD_DIM // 2), axis=1) for m in range(1, per_row)]
        for j in range(per_row):
            t = rolled[(0 - j) % per_row]
            for k in range(1, per_row):
                t = jnp.where(quarter == k, rolled[(k - j) % per_row], t)
            out_ref[pl.ds(j, tp, stride=per_row), :] = t


def _rope_tables(pos_packed, invf, *, tp=512):
    rows = pos_packed.shape[0]
    tp = min(tp, rows)
    per_row = LANES // (ATT_HEAD_DIM // 2)
    return pl.pallas_call(
        functools.partial(_rope_kernel, tp=tp),
        grid=(rows // tp,),
        in_specs=[pl.BlockSpec((tp, LANES), lambda i: (i, 0)), _const_spec((1, LANES))],
        out_specs=[pl.BlockSpec((per_row * tp, LANES), lambda i: (i, 0))] * 2,
        out_shape=[jax.ShapeDtypeStruct((per_row * rows, LANES), F32)] * 2,
        compiler_params=_params(("arbitrary",)),
        name="rope_tables",
    )(pos_packed, invf)


def _head_norm_rope(y, gain_ref, cos, sin_signed, first_half, ones_bd):
    outs = []
    w = y.shape[1]
    for c in range(0, w, 2 * LANES):
        blk = y[:, c:c + 2 * LANES]
        ss = jnp.dot((blk * blk).astype(BF16), ones_bd, preferred_element_type=F32)
        yn = blk * lax.rsqrt(ss * (1.0 / ATT_HEAD_DIM) + EPS) * gain_ref[:, c:c + 2 * LANES]
        for j in range(0, 2 * LANES, LANES):
            v = yn[:, j:j + LANES]
            lo = pltpu.roll(v, ATT_HEAD_DIM // 2, axis=1)
            hi = pltpu.roll(v, LANES - ATT_HEAD_DIM // 2, axis=1)
            rot = jnp.where(first_half, hi, lo)
            outs.append(v * cos + rot * sin_signed)
    return outs


def _dup_heads(blocks, low):
    outs = []
    for v in blocks:
        swapped = pltpu.roll(v, ATT_HEAD_DIM, axis=1)
        outs.append(jnp.where(low, v, swapped).astype(BF16))
        outs.append(jnp.where(low, swapped, v).astype(BF16))
    return jnp.concatenate(outs, axis=1)


def _qkv_kernel(x_ref, cos_ref, sin_ref, gkv_ref, gq_ref, wkv_ref, bkv_ref, wq_ref, bq_ref,
                kn_ref, qn_ref, q_ref, k_ref, v_ref, *, tm):
    x = x_ref[...]
    xn = x * _rms_scale(x)
    hkv = (xn * gkv_ref[...]).astype(BF16)
    hq = (xn * gq_ref[...]).astype(BF16)
    kw = ATT_KV_HEADS * ATT_HEAD_DIM

    lane = lax.broadcasted_iota(jnp.int32, (tm, LANES), 1)
    first_half = (lane % ATT_HEAD_DIM) < (ATT_HEAD_DIM // 2)
    low = lane < ATT_HEAD_DIM
    cos = cos_ref[...]
    sin_signed = jnp.where(first_half, -sin_ref[...], sin_ref[...])
    r = lax.broadcasted_iota(jnp.int32, (2 * LANES, 2 * LANES), 0) // ATT_HEAD_DIM
    c = lax.broadcasted_iota(jnp.int32, (2 * LANES, 2 * LANES), 1) // ATT_HEAD_DIM
    ones_bd = jnp.where(r == c, 1.0, 0.0).astype(BF16)

    kv = jnp.dot(hkv, wkv_ref[...], preferred_element_type=F32) + bkv_ref[...]
    k_ref[...] = _dup_heads(_head_norm_rope(kv[:, :kw], kn_ref, cos, sin_signed, first_half, ones_bd), low)
    v_ref[...] = _dup_heads([kv[:, kw + j:kw + j + LANES] for j in range(0, kw, LANES)], low)
    q = jnp.dot(hq, wq_ref[...], preferred_element_type=F32) + bq_ref[...]
    qs = _head_norm_rope(q, qn_ref, cos, sin_signed, first_half, ones_bd)
    q_ref[...] = jnp.concatenate([v.astype(BF16) for v in qs], axis=1)


def _qkv(x2, cos, sin, gkv, gq, wkv, bkv, wq, bq, kn, qn, *, tm=1024):
    t = x2.shape[0]
    kw = ATT_KV_HEADS * ATT_HEAD_DIM
    qw = ATT_Q_HEADS * ATT_HEAD_DIM
    row = lambda i: (i, 0)
    return pl.pallas_call(
        functools.partial(_qkv_kernel, tm=tm),
        grid=(t // tm,),
        in_specs=[
            pl.BlockSpec((tm, D_MODEL), row),
            pl.BlockSpec((tm, LANES), row),
            pl.BlockSpec((tm, LANES), row),
            _const_spec((1, D_MODEL)),
            _const_spec((1, D_MODEL)),
            _const_spec((D_MODEL, 2 * kw)),
            _const_spec((1, 2 * kw)),
            _const_spec((D_MODEL, qw)),
            _const_spec((1, qw)),
            _const_spec((1, kw)),
            _const_spec((1, qw)),
        ],
        out_specs=[
            pl.BlockSpec((tm, qw), row),
            pl.BlockSpec((tm, 2 * kw), row),
            pl.BlockSpec((tm, 2 * kw), row),
        ],
        out_shape=[
            jax.ShapeDtypeStruct((t, qw), BF16),
            jax.ShapeDtypeStruct((t, 2 * kw), BF16),
            jax.ShapeDtypeStruct((t, 2 * kw), BF16),
        ],
        compiler_params=_params(("arbitrary",)),
        name="qkv_rope",
    )(x2, cos, sin, gkv, gq, wkv, bkv, wq, bq, kn, qn)


def _attn_kernel(sink_ref, q_ref, k_ref, v_ref, x_ref, wo_ref, bo_ref, o_ref, att_ref, *, tq):
    i = pl.program_id(1)
    W = WINDOW
    G = ATT_GROUP
    low = lax.broadcasted_iota(jnp.int32, (W, LANES), 1) < ATT_HEAD_DIM
    qi = lax.broadcasted_iota(jnp.int32, (W, 2 * W), 0)
    ki = lax.broadcasted_iota(jnp.int32, (W, 2 * W), 1)
    ones = jnp.ones((2 * W, LANES), BF16)

    for jb in range(tq // W):
        q0 = i * tq + jb * W
        start = pl.multiple_of(jnp.maximum(q0 - W, 0), W)
        rel = qi + (q0 - start) - ki
        neg = jnp.where((rel >= 0) & (rel < W), 0.0, -jnp.inf)
        rows = slice(jb * W, (jb + 1) * W)
        for hk in range(ATT_KV_HEADS):
            kh = k_ref[pl.ds(start, 2 * W), hk * LANES:(hk + 1) * LANES]
            vh = v_ref[pl.ds(start, 2 * W), hk * LANES:(hk + 1) * LANES]
            parts = []
            for g in range(G):
                qb = q_ref[rows, (hk * G + g) // 2 * LANES:((hk * G + g) // 2 + 1) * LANES]
                keep = low if g % 2 == 0 else jnp.logical_not(low)
                parts.append(jnp.where(keep, qb, jnp.zeros_like(qb)))
            qst = jnp.concatenate(parts, axis=0)
            s = lax.dot_general(qst, kh, (((1,), (1,)), ((), ())), preferred_element_type=F32)
            ps, ms = [], []
            for g in range(G):
                sg = s[g * W:(g + 1) * W, :] + neg
                m = jnp.maximum(jnp.max(sg, axis=-1, keepdims=True), sink_ref[hk * G + g])
                ps.append(jnp.exp(sg - m).astype(BF16))
                ms.append(m)
            pv = jnp.dot(jnp.concatenate(ps, axis=0), jnp.concatenate([vh, ones], axis=1),
                         preferred_element_type=F32)
            outs = []
            for g in range(G):
                blk = pv[g * W:(g + 1) * W, :]
                denom = blk[:, LANES:] + jnp.exp(sink_ref[hk * G + g] - ms[g])
                outs.append(blk[:, :LANES] / denom)
            for half in range(G // 2):
                col = (hk * G) // 2 + half
                att_ref[rows, col * LANES:(col + 1) * LANES] = jnp.where(
                    low, outs[2 * half], outs[2 * half + 1]).astype(BF16)
    o_ref[...] = (x_ref[...] + bo_ref[...]
                  + jnp.dot(att_ref[...], wo_ref[...], preferred_element_type=F32))


def _attn(sinks, q, k, v, x2, wo, bo, *, batch, seq, tq=512):
    nst = seq // tq
    qw = ATT_Q_HEADS * ATT_HEAD_DIM
    kw = ATT_KV_HEADS * ATT_HEAD_DIM
    row = lambda b, i: (b * nst + i, 0)
    per_batch = lambda b, i: (b, 0)
    t = batch * seq
    return pl.pallas_call(
        functools.partial(_attn_kernel, tq=tq),
        grid=(batch, nst),
        in_specs=[
            pl.BlockSpec(memory_space=pltpu.SMEM),
            pl.BlockSpec((tq, qw), row),
            pl.BlockSpec((seq, 2 * kw), per_batch),
            pl.BlockSpec((seq, 2 * kw), per_batch),
            pl.BlockSpec((tq, D_MODEL), row),
            _const_spec((qw, D_MODEL)),
            _const_spec((1, D_MODEL)),
        ],
        out_specs=pl.BlockSpec((tq, D_MODEL), row),
        out_shape=jax.ShapeDtypeStruct((t, D_MODEL), F32),
        scratch_shapes=[pltpu.VMEM((tq, qw), BF16)],
        compiler_params=_params(("arbitrary", "arbitrary")),
        name="swa_attn",
    )(sinks, q, k, v, x2, wo, bo)


def _row(v, width=None):
    v = v.reshape(1, -1).astype(F32)
    if width is not None and v.shape[1] < width:
        v = jnp.pad(v, ((0, 0), (0, width - v.shape[1])))
    return v


def kernel(x, positions, a_norm, a_in_proj, a_conv_w, a_conv_b, a_dt_bias, a_A_log, a_D, a_gnorm,
           a_out_proj, kv_norm, w_kv, b_kv, k_norm, b_norm, w_q, b_q, q_norm, sinks, w_o, b_o,
           f_norm, f_w_in, f_conv_w, f_conv_b, f_w_down):
    batch, seq, _ = x.shape
    t = batch * seq
    x2 = x.reshape(t, D_MODEL)

    w_in = a_in_proj.reshape(D_MODEL, SSM_PROJ)
    w_dt = jnp.pad(w_in[:, SSM_INNER + SSM_XBC:], ((0, 0), (0, LANES - SSM_HEADS))).astype(BF16)
    w_main = w_in.astype(BF16)
    z, xbc, dt = _mamba_in(
        x2, _row(a_norm[0]), w_main, w_dt, 0.5 * a_conv_w[0].astype(F32), 0.5 * _row(a_conv_b[0]),
        _row(a_dt_bias[0], LANES), batch=batch, seq=seq)
    dx = jnp.repeat(a_D[0].astype(F32), SSM_HEAD_DIM).reshape(1, SSM_INNER)
    wo_ssm = (a_gnorm[0].astype(F32)[:, None] * a_out_proj.reshape(SSM_INNER, D_MODEL)).astype(BF16)
    later = [(f_w_in, 0), (f_w_down, 0), (f_w_in, 1), (f_w_down, 1), (w_kv, None),
             (w_q.reshape(D_MODEL, -1), None), (w_o.reshape(-1, D_MODEL), None)]
    x2, win0, wd0, win1, wd1, wkv_b, wq_b, wo_b = _ssd(
        xbc, z, dt, x2, _row(a_A_log[0], LANES), dx, wo_ssm, batch=batch, seq=seq, casts=later)
    f_cw = 0.5 * f_conv_w.astype(F32)
    f_cb = 0.5 * f_conv_b.astype(F32)
    (x2,) = _ffn(x2, _row(f_norm[0]), win0, f_cw[0], _row(f_cb[0]), wd0, batch=batch, seq=seq)

    half = ATT_HEAD_DIM // 2
    inv_freq = ROPE_THETA ** (-jnp.arange(half, dtype=F32) / half)
    invf = jnp.tile(inv_freq, LANES // half).reshape(1, LANES)
    per_row = LANES // half
    pos_packed = jnp.repeat(positions.reshape(t // per_row, per_row), half, axis=1)
    cos, sin = _rope_tables(pos_packed, invf)
    q_gain = jnp.tile(q_norm[0].astype(F32), ATT_Q_HEADS) * (ATT_HEAD_DIM ** -0.5)
    q, k, v = _qkv(x2, cos, sin, _row(kv_norm), _row(b_norm[0]), wkv_b, _row(b_kv), wq_b, _row(b_q[0]),
                   _row(jnp.tile(k_norm, ATT_KV_HEADS)), _row(q_gain))
    x2 = _attn(sinks[0].astype(F32), q, k, v, x2, wo_b, _row(b_o[0]), batch=batch, seq=seq)
    (x2,) = _ffn(x2, _row(f_norm[1]), win1, f_cw[1], _row(f_cb[1]), wd1, batch=batch, seq=seq)
    return x2.reshape(batch, seq, D_MODEL)
```

```python
import functools

import jax
import jax.numpy as jnp
from jax import lax
from jax.experimental import pallas as pl
from jax.experimental.pallas import tpu as pltpu

F32 = jnp.float32
BF16 = jnp.bfloat16

D_MODEL = 1024
EPS = 1e-5

SSM_INNER = 2048
SSM_HEAD_DIM = 64
SSM_HEADS = 32
SSM_GROUPS = 8
SSM_STATE = 128
SSM_CONV = 4
SSD_BLOCK = 128
LOG2E = 1.4426950408889634
SSD_GROUP_BATCH = 1
SSD_STAGE_LAGS = (0, 0, 2)
SSM_XBC = SSM_INNER + 2 * SSM_GROUPS * SSM_STATE
SSM_GROUP_WIDTH = SSM_INNER // SSM_GROUPS
SSM_PROJ = SSM_INNER + SSM_XBC + SSM_HEADS

ATT_HEAD_DIM = 64
ATT_Q_HEADS = 16
ATT_KV_HEADS = 4
ATT_GROUP = 4
WINDOW = 128
ROPE_THETA = 10000.0

FFN_DIM = 2816
FFN_CONV = 3

LANES = 128
CARRY_ROWS = 8
BF16_ROWS = 16
VMEM_LIMIT = 60 * 1024 * 1024


def _params(sem):
    return pltpu.CompilerParams(dimension_semantics=sem, vmem_limit_bytes=VMEM_LIMIT)


def _const_spec(shape):
    nd = len(shape)
    return pl.BlockSpec(shape, lambda *_: (0,) * nd, pipeline_mode=pl.Buffered(1))


CAST_BLOCKS = 16


def _cast_jobs(items, n_steps, step_index):
    inputs, in_specs, out_specs, out_shapes = [], [], [], []
    if items:
        assert n_steps % CAST_BLOCKS == 0, n_steps
    per_block = n_steps // CAST_BLOCKS
    block_of = lambda *g: step_index(*g) // per_block
    for arr, layer in items:
        r, c = arr.shape[-2:]
        rows = r // CAST_BLOCKS
        assert rows * CAST_BLOCKS == r and rows % BF16_ROWS == 0, arr.shape
        if layer is None:
            in_specs.append(pl.BlockSpec((rows, c), lambda *g: (block_of(*g), 0)))
        else:
            in_specs.append(pl.BlockSpec((None, rows, c), lambda *g, layer=layer: (layer, block_of(*g), 0)))
        out_specs.append(pl.BlockSpec((rows, c), lambda *g: (block_of(*g), 0)))
        out_shapes.append(jax.ShapeDtypeStruct((r, c), BF16))
        inputs.append(arr)
    return inputs, in_specs, out_specs, out_shapes


def _run_cast_jobs(cast_in, cast_out):
    for src, dst in zip(cast_in, cast_out):
        dst[...] = src[...].astype(BF16)


def _rms_scale(x):
    return lax.rsqrt(jnp.mean(x * x, axis=-1, keepdims=True) + EPS)


def _silu_of_half(hx):
    return hx + hx * jnp.tanh(hx)


def _causal_conv(pre, hist_ref, w_ref, b_ref, c0, width, taps):
    tm = pre.shape[0]
    cols = slice(c0, c0 + width)
    ext = jnp.concatenate([hist_ref[:, cols], pre], axis=0)
    hist_ref[:, cols] = pre[tm - CARRY_ROWS:tm, :]
    w = [w_ref[k:k + 1, cols] for k in range(taps)]
    back1 = pltpu.roll(ext, 1, axis=0)
    if taps == 4:
        near = w[3] * ext + w[2] * back1
        far = w[1] * ext + w[0] * back1
        out = near + pltpu.roll(far, 2, axis=0)
    else:
        assert taps == 3
        far = w[1] * ext + w[0] * back1
        out = w[2] * ext + pltpu.roll(far, 1, axis=0)
    return out[CARRY_ROWS:, :] + b_ref[:, cols]


def _mamba_in_kernel(*refs, chunk, n_cast):
    x_ref, g_ref, w_ref, wdt_ref, cw_ref, cb_ref, dtb_ref = refs[:7]
    cast_in = refs[7:7 + n_cast]
    z_ref, xbc_ref, dt_ref = refs[7 + n_cast:10 + n_cast]
    cast_out = refs[10 + n_cast:10 + 2 * n_cast]
    (hist_ref,) = refs[10 + 2 * n_cast:]

    @pl.when(pl.program_id(1) == 0)
    def _():
        hist_ref[...] = jnp.zeros_like(hist_ref)

    x = x_ref[...]
    h = (x * _rms_scale(x) * g_ref[...]).astype(BF16)
    z_chunks = list(range(0, SSM_INNER, chunk))
    for i, c in enumerate(range(0, SSM_XBC, chunk)):
        pre = jnp.dot(h, w_ref[:, SSM_INNER + c:SSM_INNER + c + chunk], preferred_element_type=F32)
        conv = _causal_conv(pre, hist_ref, cw_ref, cb_ref, c, chunk, SSM_CONV)
        xbc_ref[:, c:c + chunk] = _silu_of_half(conv.astype(BF16))
        if i % 2 == 1:
            zc = z_chunks[i // 2]
            z_ref[:, zc:zc + chunk] = jnp.dot(h, w_ref[:, zc:zc + chunk], preferred_element_type=F32).astype(BF16)
    dtp = jnp.dot(h, wdt_ref[...], preferred_element_type=F32) + dtb_ref[...]
    dt_ref[...] = jnp.maximum(dtp, 0.0) + jnp.log1p(jnp.exp(-jnp.abs(dtp)))
    _run_cast_jobs(cast_in, cast_out)


def _mamba_in(x2, g, w, wdt, cw, cb, dtb, *, batch, seq, tm=1024, chunk=256, casts=()):
    nst = seq // tm
    row = lambda b, i: (b * nst + i, 0)
    t = batch * seq
    c_in, c_in_specs, c_out_specs, c_out_shapes = _cast_jobs(casts, batch * nst, lambda b, i: b * nst + i)
    return pl.pallas_call(
        functools.partial(_mamba_in_kernel, chunk=chunk, n_cast=len(c_in)),
        grid=(batch, nst),
        in_specs=[
            pl.BlockSpec((tm, D_MODEL), row),
            _const_spec((1, D_MODEL)),
            _const_spec((D_MODEL, SSM_PROJ)),
            _const_spec((D_MODEL, LANES)),
            _const_spec((SSM_CONV, SSM_XBC)),
            _const_spec((1, SSM_XBC)),
            _const_spec((1, LANES)),
        ] + c_in_specs,
        out_specs=[
            pl.BlockSpec((tm, SSM_INNER), row),
            pl.BlockSpec((tm, SSM_XBC), row),
            pl.BlockSpec((tm, LANES), row),
        ] + c_out_specs,
        out_shape=[
            jax.ShapeDtypeStruct((t, SSM_INNER), BF16),
            jax.ShapeDtypeStruct((t, SSM_XBC), BF16),
            jax.ShapeDtypeStruct((t, LANES), F32),
        ] + c_out_shapes,
        scratch_shapes=[pltpu.VMEM((CARRY_ROWS, SSM_XBC), F32)],
        compiler_params=_params(("arbitrary", "arbitrary")),
        name="mamba_in",
    )(x2, g, w, wdt, cw, cb, dtb, *c_in)


def _split3(a):
    hi = a.astype(BF16)
    r1 = a - hi.astype(F32)
    mid = r1.astype(BF16)
    lo = (r1 - mid.astype(F32)).astype(BF16)
    return hi, mid, lo


def _pack3(v, valid):
    hi, mid, lo = _split3(jnp.where(valid, v, 0.0))
    packed = (hi.astype(F32) + pltpu.roll(mid.astype(F32), SSM_HEADS, axis=1)
              + pltpu.roll(lo.astype(F32), 2 * SSM_HEADS, axis=1))
    return packed.astype(BF16)


def _ssd_kernel(*refs, ts, n_cast):
    xbc_ref, z_ref, dt_ref, x_ref, alog_ref, dx_ref, wo_ref = refs[:7]
    cast_in = refs[7:7 + n_cast]
    o_ref = refs[7 + n_cast]
    cast_out = refs[8 + n_cast:8 + 2 * n_cast]
    state_ref, yn_ref, sel_ref, sel64_ref = refs[8 + 2 * n_cast:]
    L = SSD_BLOCK
    GW = SSM_GROUP_WIDTH
    HPG = SSM_HEADS // SSM_GROUPS
    HD = SSM_HEAD_DIM
    b_off = SSM_INNER
    c_off = SSM_INNER + SSM_GROUPS * SSM_STATE

    @pl.when(pl.program_id(1) == 0)
    def _():
        state_ref[...] = jnp.zeros_like(state_ref)

    @pl.when((pl.program_id(0) == 0) & (pl.program_id(1) == 0))
    def _():
        for g in range(SSM_GROUPS):
            for ref, width in ((sel_ref, LANES), (sel64_ref, HD)):
                r = lax.broadcasted_iota(jnp.int32, (LANES, HPG * width), 0)
                e = lax.broadcasted_iota(jnp.int32, (LANES, HPG * width), 1) // width
                hit = (r < 3 * SSM_HEADS) & ((r % SSM_HEADS) == g * HPG + e)
                ref[g] = jnp.where(hit, 1.0, 0.0).astype(BF16)

    a_scale = -jnp.exp(alog_ref[...]) * LOG2E
    row = lax.broadcasted_iota(jnp.int32, (L, L), 0)
    col = lax.broadcasted_iota(jnp.int32, (L, L), 1)
    causal = row >= col
    tri = jnp.where(causal, 1.0, 0.0).astype(BF16)
    lane = lax.broadcasted_iota(jnp.int32, (L, LANES), 1)
    valid = lane < SSM_HEADS
    low = lane < HD
    lane_grp = lax.broadcasted_iota(jnp.int32, (L, GW), 1) // HD

    def block(c):
        r0 = pl.multiple_of(c * L, L)
        rows = pl.ds(r0, L)
        dt = dt_ref[rows, :]
        a = dt * a_scale
        acum = None
        for term in _split3(a):
            part = jnp.dot(tri, term, preferred_element_type=F32)
            acum = part if acum is None else acum + part
        acum_t = jnp.transpose(acum)
        acum_p = _pack3(acum, valid)
        dt_p = _pack3(dt, valid)

        batches = [range(g0, g0 + SSD_GROUP_BATCH) for g0 in range(0, SSM_GROUPS, SSD_GROUP_BATCH)]
        stages = (scan_load, scan_mid, scan_back)
        ctx = {}
        for t in range(len(batches) + max(SSD_STAGE_LAGS)):
            for stage, lag in zip(stages, SSD_STAGE_LAGS):
                j = t - lag
                if 0 <= j < len(batches):
                    ctx[j] = stage(rows, acum_t, acum_p, dt_p, batches[j], ctx.get(j))

    def scan_load(rows, acum_t, acum_p, dt_p, G, _):
        xs = {g: xbc_ref[rows, g * GW:(g + 1) * GW].astype(F32) for g in G}
        cg = {g: xbc_ref[rows, c_off + g * SSM_STATE:c_off + (g + 1) * SSM_STATE] for g in G}
        bgt = {g: jnp.transpose(xbc_ref[rows, b_off + g * SSM_STATE:b_off + (g + 1) * SSM_STATE]
                                .astype(F32)).astype(BF16) for g in G}
        acol = {g: jnp.dot(acum_p, sel_ref[g], preferred_element_type=F32) for g in G}
        dt_x = {g: jnp.dot(dt_p, sel64_ref[g], preferred_element_type=F32) for g in G}
        cb = {g: jnp.dot(cg[g], bgt[g], preferred_element_type=F32) for g in G}
        st = {g: state_ref[g] for g in G}
        y_in = {g: jnp.dot(cg[g], st[g].astype(BF16), preferred_element_type=F32) for g in G}
        return xs, bgt, st, y_in, acol, dt_x, cb

    def scan_mid(rows, acum_t, acum_p, dt_p, G, ctx):
        xs, bgt, st, y_in, acol, dt_x, cb = ctx
        xd, wcat, xcat, acum_x = {}, {}, {}, {}
        for g in G:
            ab = [acol[g][:, e * LANES:(e + 1) * LANES] for e in range(HPG)]
            acum_x[g] = jnp.concatenate([jnp.where(low, ab[0], ab[1]), jnp.where(low, ab[2], ab[3])], axis=1)
            xd[g] = xs[g] * dt_x[g]
            xd_b = xd[g].astype(BF16)
            ws, xparts = [], []
            for e in range(HPG):
                xparts.append(jnp.where(lane_grp == e, xd_b, jnp.zeros_like(xd_b)))
                seg = ab[e] - acum_t[g * HPG + e:g * HPG + e + 1, :]
                ws.append((cb[g] * jnp.exp2(jnp.where(causal, seg, -jnp.inf))).astype(BF16))
            wcat[g] = jnp.concatenate(ws, axis=1)
            xcat[g] = jnp.concatenate(xparts, axis=0)
        y = {g: jnp.dot(wcat[g], xcat[g], preferred_element_type=F32) for g in G}
        return xs, bgt, st, y_in, xd, acum_x, y

    def scan_back(rows, acum_t, acum_p, dt_p, G, ctx):
        xs, bgt, st, y_in, xd, acum_x, y = ctx
        to_end, alast = {}, {}
        for g in G:
            alast[g] = acum_x[g][L - 1:L, :]
            alast_x = alast[g]
            y[g] = y[g] + y_in[g] * jnp.exp2(acum_x[g])
            to_end[g] = (xd[g] * jnp.exp2(alast_x - acum_x[g])).astype(BF16)
        for g in G:
            alast_x = alast[g]
            state_ref[g] = st[g] * jnp.exp2(alast_x) + jnp.dot(bgt[g], to_end[g], preferred_element_type=F32)
        for g in G:
            yg = y[g] + xs[g] * dx_ref[:, g * GW:(g + 1) * GW]
            yg = yg * _silu_of_half(0.5 * z_ref[rows, g * GW:(g + 1) * GW].astype(F32))
            yn = yg * _rms_scale(yg)
            yn_ref[rows, g * GW:(g + 1) * GW] = yn.astype(BF16)

    def step(c, carry):
        block(c)
        return carry

    lax.fori_loop(0, ts // L, step, 0)
    o_ref[...] = x_ref[...] + jnp.dot(yn_ref[...], wo_ref[...], preferred_element_type=F32)
    _run_cast_jobs(cast_in, cast_out)


def _ssd(xbc, z, dt, x2, alog, dx, wo, *, batch, seq, ts=512, casts=()):
    nst = seq // ts
    row = lambda b, i: (b * nst + i, 0)
    t = batch * seq
    c_in, c_in_specs, c_out_specs, c_out_shapes = _cast_jobs(casts, batch * nst, lambda b, i: b * nst + i)
    return pl.pallas_call(
        functools.partial(_ssd_kernel, ts=ts, n_cast=len(c_in)),
        grid=(batch, nst),
        in_specs=[
            pl.BlockSpec((ts, SSM_XBC), row),
            pl.BlockSpec((ts, SSM_INNER), row),
            pl.BlockSpec((ts, LANES), row),
            pl.BlockSpec((ts, D_MODEL), row),
            _const_spec((1, LANES)),
            _const_spec((1, SSM_INNER)),
            _const_spec((SSM_INNER, D_MODEL)),
        ] + c_in_specs,
        out_specs=[pl.BlockSpec((ts, D_MODEL), row)] + c_out_specs,
        out_shape=[jax.ShapeDtypeStruct((t, D_MODEL), F32)] + c_out_shapes,
        scratch_shapes=[
            pltpu.VMEM((SSM_GROUPS, SSM_STATE, SSM_GROUP_WIDTH), F32),
            pltpu.VMEM((ts, SSM_INNER), BF16),
            pltpu.VMEM((SSM_GROUPS, LANES, (SSM_HEADS // SSM_GROUPS) * LANES), BF16),
            pltpu.VMEM((SSM_GROUPS, LANES, SSM_GROUP_WIDTH), BF16),
        ],
        compiler_params=_params(("arbitrary", "arbitrary")),
        name="ssd_out",
    )(xbc, z, dt, x2, alog, dx, wo, *c_in)


_FFN_CHUNKS = ((0, 512), (512, 512), (1024, 512), (1536, 512), (2048, 512), (2560, 256))


def _ffn_kernel(*refs, n_cast):
    x_ref, g_ref, win_ref, cw_ref, cb_ref, wd_ref = refs[:6]
    cast_in = refs[6:6 + n_cast]
    o_ref = refs[6 + n_cast]
    cast_out = refs[7 + n_cast:7 + 2 * n_cast]
    hist_ref, hid_ref = refs[7 + 2 * n_cast:]

    @pl.when(pl.program_id(1) == 0)
    def _():
        hist_ref[...] = jnp.zeros_like(hist_ref)

    x = x_ref[...]
    h = (x * _rms_scale(x) * g_ref[...]).astype(BF16)
    for c, width in _FFN_CHUNKS:
        gate = jnp.dot(h, win_ref[:, c:c + width], preferred_element_type=F32)
        val = jnp.dot(h, win_ref[:, FFN_DIM + c:FFN_DIM + c + width], preferred_element_type=F32)
        conv = _causal_conv(gate, hist_ref, cw_ref, cb_ref, c, width, FFN_CONV)
        hid_ref[:, c:c + width] = (_silu_of_half(conv) * val).astype(BF16)
    o_ref[...] = x + jnp.dot(hid_ref[...], wd_ref[...], preferred_element_type=F32)
    _run_cast_jobs(cast_in, cast_out)


def _ffn(x2, g, win, cw, cb, wd, *, batch, seq, tm=1024, casts=()):
    nst = seq // tm
    row = lambda b, i: (b * nst + i, 0)
    t = batch * seq
    c_in, c_in_specs, c_out_specs, c_out_shapes = _cast_jobs(casts, batch * nst, lambda b, i: b * nst + i)
    return pl.pallas_call(
        functools.partial(_ffn_kernel, n_cast=len(c_in)),
        grid=(batch, nst),
        in_specs=[
            pl.BlockSpec((tm, D_MODEL), row),
            _const_spec((1, D_MODEL)),
            _const_spec((D_MODEL, 2 * FFN_DIM)),
            _const_spec((FFN_CONV, FFN_DIM)),
            _const_spec((1, FFN_DIM)),
            _const_spec((FFN_DIM, D_MODEL)),
        ] + c_in_specs,
        out_specs=[pl.BlockSpec((tm, D_MODEL), row)] + c_out_specs,
        out_shape=[jax.ShapeDtypeStruct((t, D_MODEL), F32)] + c_out_shapes,
        scratch_shapes=[
            pltpu.VMEM((CARRY_ROWS, FFN_DIM), F32),
            pltpu.VMEM((tm, FFN_DIM), BF16),
        ],
        compiler_params=_params(("arbitrary", "arbitrary")),
        name="conv_ffn",
    )(x2, g, win, cw, cb, wd, *c_in)


def _rope_kernel(pos_ref, invf_ref, cos_ref, sin_ref, *, tp):
    per_row = LANES // (ATT_HEAD_DIM // 2)
    ang = pos_ref[...].astype(F32) * invf_ref[...]
    quarter = lax.broadcasted_iota(jnp.int32, (tp, LANES), 1) // (ATT_HEAD_DIM // 2)
    for arr, out_ref in ((jnp.cos(ang), cos_ref), (jnp.sin(ang), sin_ref)):
        rolled = [arr] + [pltpu.roll(arr, m * (ATT_HEAD_DIM // 2), axis=1) for m in range(1, per_row)]
        for j in range(per_row):
            t = rolled[(0 - j) % per_row]
            for k in range(1, per_row):
                t = jnp.where(quarter == k, rolled[(k - j) % per_row], t)
            out_ref[pl.ds(j, tp, stride=per_row), :] = t


def _rope_tables(pos_packed, invf, *, tp=512):
    rows = pos_packed.shape[0]
    tp = min(tp, rows)
    per_row = LANES // (ATT_HEAD_DIM // 2)
    return pl.pallas_call(
        functools.partial(_rope_kernel, tp=tp),
        grid=(rows // tp,),
        in_specs=[pl.BlockSpec((tp, LANES), lambda i: (i, 0)), _const_spec((1, LANES))],
        out_specs=[pl.BlockSpec((per_row * tp, LANES), lambda i: (i, 0))] * 2,
        out_shape=[jax.ShapeDtypeStruct((per_row * rows, LANES), F32)] * 2,
        compiler_params=_params(("arbitrary",)),
        name="rope_tables",
    )(pos_packed, invf)


def _head_norm_rope(y, gain_ref, cos, sin_signed, first_half, ones_bd):
    outs = []
    w = y.shape[1]
    for c in range(0, w, 2 * LANES):
        blk = y[:, c:c + 2 * LANES]
        ss = jnp.dot((blk * blk).astype(BF16), ones_bd, preferred_element_type=F32)
        yn = blk * lax.rsqrt(ss * (1.0 / ATT_HEAD_DIM) + EPS) * gain_ref[:, c:c + 2 * LANES]
        for j in range(0, 2 * LANES, LANES):
            v = yn[:, j:j + LANES]
            lo = pltpu.roll(v, ATT_HEAD_DIM // 2, axis=1)
            hi = pltpu.roll(v, LANES - ATT_HEAD_DIM // 2, axis=1)
            rot = jnp.where(first_half, hi, lo)
            outs.append(v * cos + rot * sin_signed)
    return outs


def _dup_heads(blocks, low):
    outs = []
    for v in blocks:
        swapped = pltpu.roll(v, ATT_HEAD_DIM, axis=1)
        outs.append(jnp.where(low, v, swapped).astype(BF16))
        outs.append(jnp.where(low, swapped, v).astype(BF16))
    return jnp.concatenate(outs, axis=1)


def _qkv_kernel(x_ref, cos_ref, sin_ref, gkv_ref, gq_ref, wkv_ref, bkv_ref, wq_ref, bq_ref,
                kn_ref, qn_ref, q_ref, k_ref, v_ref, *, tm):
    x = x_ref[...]
    xn = x * _rms_scale(x)
    hkv = (xn * gkv_ref[...]).astype(BF16)
    hq = (xn * gq_ref[...]).astype(BF16)
    kw = ATT_KV_HEADS * ATT_HEAD_DIM

    lane = lax.broadcasted_iota(jnp.int32, (tm, LANES), 1)
    first_half = (lane % ATT_HEAD_DIM) < (ATT_HEAD_DIM // 2)
    low = lane < ATT_HEAD_DIM
    cos = cos_ref[...]
    sin_signed = jnp.where(first_half, -sin_ref[...], sin_ref[...])
    r = lax.broadcasted_iota(jnp.int32, (2 * LANES, 2 * LANES), 0) // ATT_HEAD_DIM
    c = lax.broadcasted_iota(jnp.int32, (2 * LANES, 2 * LANES), 1) // ATT_HEAD_DIM
    ones_bd = jnp.where(r == c, 1.0, 0.0).astype(BF16)

    kv = jnp.dot(hkv, wkv_ref[...], preferred_element_type=F32) + bkv_ref[...]
    k_ref[...] = _dup_heads(_head_norm_rope(kv[:, :kw], kn_ref, cos, sin_signed, first_half, ones_bd), low)
    v_ref[...] = _dup_heads([kv[:, kw + j:kw + j + LANES] for j in range(0, kw, LANES)], low)
    q = jnp.dot(hq, wq_ref[...], preferred_element_type=F32) + bq_ref[...]
    qs = _head_norm_rope(q, qn_ref, cos, sin_signed, first_half, ones_bd)
    q_ref[...] = jnp.concatenate([v.astype(BF16) for v in qs], axis=1)


def _qkv(x2, cos, sin, gkv, gq, wkv, bkv, wq, bq, kn, qn, *, tm=1024):
    t = x2.shape[0]
    kw = ATT_KV_HEADS * ATT_HEAD_DIM
    qw = ATT_Q_HEADS * ATT_HEAD_DIM
    row = lambda i: (i, 0)
    return pl.pallas_call(
        functools.partial(_qkv_kernel, tm=tm),
        grid=(t // tm,),
        in_specs=[
            pl.BlockSpec((tm, D_MODEL), row),
            pl.BlockSpec((tm, LANES), row),
            pl.BlockSpec((tm, LANES), row),
            _const_spec((1, D_MODEL)),
            _const_spec((1, D_MODEL)),
            _const_spec((D_MODEL, 2 * kw)),
            _const_spec((1, 2 * kw)),
            _const_spec((D_MODEL, qw)),
            _const_spec((1, qw)),
            _const_spec((1, kw)),
            _const_spec((1, qw)),
        ],
        out_specs=[
            pl.BlockSpec((tm, qw), row),
            pl.BlockSpec((tm, 2 * kw), row),
            pl.BlockSpec((tm, 2 * kw), row),
        ],
        out_shape=[
            jax.ShapeDtypeStruct((t, qw), BF16),
            jax.ShapeDtypeStruct((t, 2 * kw), BF16),
            jax.ShapeDtypeStruct((t, 2 * kw), BF16),
        ],
        compiler_params=_params(("arbitrary",)),
        name="qkv_rope",
    )(x2, cos, sin, gkv, gq, wkv, bkv, wq, bq, kn, qn)


def _attn_kernel(sink_ref, q_ref, k_ref, v_ref, x_ref, wo_ref, bo_ref, o_ref, att_ref, *, tq):
    i = pl.program_id(1)
    W = WINDOW
    G = ATT_GROUP
    low = lax.broadcasted_iota(jnp.int32, (W, LANES), 1) < ATT_HEAD_DIM
    qi = lax.broadcasted_iota(jnp.int32, (W, 2 * W), 0)
    ki = lax.broadcasted_iota(jnp.int32, (W, 2 * W), 1)
    ones = jnp.ones((2 * W, LANES), BF16)

    for jb in range(tq // W):
        q0 = i * tq + jb * W
        start = pl.multiple_of(jnp.maximum(q0 - W, 0), W)
        rel = qi + (q0 - start) - ki
        neg = jnp.where((rel >= 0) & (rel < W), 0.0, -jnp.inf)
        rows = slice(jb * W, (jb + 1) * W)
        for hk in range(ATT_KV_HEADS):
            kh = k_ref[pl.ds(start, 2 * W), hk * LANES:(hk + 1) * LANES]
            vh = v_ref[pl.ds(start, 2 * W), hk * LANES:(hk + 1) * LANES]
            parts = []
            for g in range(G):
                qb = q_ref[rows, (hk * G + g) // 2 * LANES:((hk * G + g) // 2 + 1) * LANES]
                keep = low if g % 2 == 0 else jnp.logical_not(low)
                parts.append(jnp.where(keep, qb, jnp.zeros_like(qb)))
            qst = jnp.concatenate(parts, axis=0)
            s = lax.dot_general(qst, kh, (((1,), (1,)), ((), ())), preferred_element_type=F32)
            ps, ms = [], []
            for g in range(G):
                sg = s[g * W:(g + 1) * W, :] + neg
                m = jnp.maximum(jnp.max(sg, axis=-1, keepdims=True), sink_ref[hk * G + g])
                ps.append(jnp.exp(sg - m).astype(BF16))
                ms.append(m)
            pv = jnp.dot(jnp.concatenate(ps, axis=0), jnp.concatenate([vh, ones], axis=1),
                         preferred_element_type=F32)
            outs = []
            for g in range(G):
                blk = pv[g * W:(g + 1) * W, :]
                denom = blk[:, LANES:] + jnp.exp(sink_ref[hk * G + g] - ms[g])
                outs.append(blk[:, :LANES] / denom)
            for half in range(G // 2):
                col = (hk * G) // 2 + half
                att_ref[rows, col * LANES:(col + 1) * LANES] = jnp.where(
                    low, outs[2 * half], outs[2 * half + 1]).astype(BF16)
    o_ref[...] = (x_ref[...] + bo_ref[...]
                  + jnp.dot(att_ref[...], wo_ref[...], preferred_element_type=F32))


def _attn(sinks, q, k, v, x2, wo, bo, *, batch, seq, tq=512):
    nst = seq // tq
    qw = ATT_Q_HEADS * ATT_HEAD_DIM
    kw = ATT_KV_HEADS * ATT_HEAD_DIM
    row = lambda b, i: (b * nst + i, 0)
    per_batch = lambda b, i: (b, 0)
    t = batch * seq
    return pl.pallas_call(
        functools.partial(_attn_kernel, tq=tq),
        grid=(batch, nst),
        in_specs=[
            pl.BlockSpec(memory_space=pltpu.SMEM),
            pl.BlockSpec((tq, qw), row),
            pl.BlockSpec((seq, 2 * kw), per_batch),
            pl.BlockSpec((seq, 2 * kw), per_batch),
            pl.BlockSpec((tq, D_MODEL), row),
            _const_spec((qw, D_MODEL)),
            _const_spec((1, D_MODEL)),
        ],
        out_specs=pl.BlockSpec((tq, D_MODEL), row),
        out_shape=jax.ShapeDtypeStruct((t, D_MODEL), F32),
        scratch_shapes=[pltpu.VMEM((tq, qw), BF16)],
        compiler_params=_params(("arbitrary", "arbitrary")),
        name="swa_attn",
    )(sinks, q, k, v, x2, wo, bo)


def _row(v, width=None):
    v = v.reshape(1, -1).astype(F32)
    if width is not None and v.shape[1] < width:
        v = jnp.pad(v, ((0, 0), (0, width - v.shape[1])))
    return v


def kernel(x, positions, a_norm, a_in_proj, a_conv_w, a_conv_b, a_dt_bias, a_A_log, a_D, a_gnorm,
           a_out_proj, kv_norm, w_kv, b_kv, k_norm, b_norm, w_q, b_q, q_norm, sinks, w_o, b_o,
           f_norm, f_w_in, f_conv_w, f_conv_b, f_w_down):
    batch, seq, _ = x.shape
    t = batch * seq
    x2 = x.reshape(t, D_MODEL)

    w_in = a_in_proj.reshape(D_MODEL, SSM_PROJ)
    w_dt = jnp.pad(w_in[:, SSM_INNER + SSM_XBC:], ((0, 0), (0, LANES - SSM_HEADS))).astype(BF16)
    w_main = w_in.astype(BF16)
    z, xbc, dt = _mamba_in(
        x2, _row(a_norm[0]), w_main, w_dt, 0.5 * a_conv_w[0].astype(F32), 0.5 * _row(a_conv_b[0]),
        _row(a_dt_bias[0], LANES), batch=batch, seq=seq)
    dx = jnp.repeat(a_D[0].astype(F32), SSM_HEAD_DIM).reshape(1, SSM_INNER)
    wo_ssm = (a_gnorm[0].astype(F32)[:, None] * a_out_proj.reshape(SSM_INNER, D_MODEL)).astype(BF16)
    later = [(f_w_in, 0), (f_w_down, 0), (f_w_in, 1), (f_w_down, 1), (w_kv, None),
             (w_q.reshape(D_MODEL, -1), None), (w_o.reshape(-1, D_MODEL), None)]
    x2, win0, wd0, win1, wd1, wkv_b, wq_b, wo_b = _ssd(
        xbc, z, dt, x2, _row(a_A_log[0], LANES), dx, wo_ssm, batch=batch, seq=seq, casts=later)
    f_cw = 0.5 * f_conv_w.astype(F32)
    f_cb = 0.5 * f_conv_b.astype(F32)
    (x2,) = _ffn(x2, _row(f_norm[0]), win0, f_cw[0], _row(f_cb[0]), wd0, batch=batch, seq=seq)

    half = ATT_HEAD_DIM // 2
    inv_freq = ROPE_THETA ** (-jnp.arange(half, dtype=F32) / half)
    invf = jnp.tile(inv_freq, LANES // half).reshape(1, LANES)
    per_row = LANES // half
    pos_packed = jnp.repeat(positions.reshape(t // per_row, per_row), half, axis=1)
    cos, sin = _rope_tables(pos_packed, invf)
    q_gain = jnp.tile(q_norm[0].astype(F32), ATT_Q_HEADS) * (ATT_HEAD_DIM ** -0.5)
    q, k, v = _qkv(x2, cos, sin, _row(kv_norm), _row(b_norm[0]), wkv_b, _row(b_kv), wq_b, _row(b_q[0]),
                   _row(jnp.tile(k_norm, ATT_KV_HEADS)), _row(q_gain))
    x2 = _attn(sinks[0].astype(F32), q, k, v, x2, wo_b, _row(b_o[0]), batch=batch, seq=seq)
    (x2,) = _ffn(x2, _row(f_norm[1]), win1, f_cw[1], _row(f_cb[1]), wd1, batch=batch, seq=seq)
    return x2.reshape(batch, seq, D_MODEL)
```

```python
import functools

import jax
import jax.numpy as jnp
from jax import lax
from jax.experimental import pallas as pl
from jax.experimental.pallas import tpu as pltpu

F32 = jnp.float32
BF16 = jnp.bfloat16

D_MODEL = 1024
EPS = 1e-5

SSM_INNER = 2048
SSM_HEAD_DIM = 64
SSM_HEADS = 32
SSM_GROUPS = 8
SSM_STATE = 128
SSM_CONV = 4
SSD_BLOCK = 128
LOG2E = 1.4426950408889634
SSD_GROUP_BATCH = 1
SSD_STAGE_LAGS = (0, 0, 2)
SSM_XBC = SSM_INNER + 2 * SSM_GROUPS * SSM_STATE
SSM_GROUP_WIDTH = SSM_INNER // SSM_GROUPS
SSM_PROJ = SSM_INNER + SSM_XBC + SSM_HEADS

ATT_HEAD_DIM = 64
ATT_Q_HEADS = 16
ATT_KV_HEADS = 4
ATT_GROUP = 4
WINDOW = 128
ROPE_THETA = 10000.0

FFN_DIM = 2816
FFN_CONV = 3

LANES = 128
CARRY_ROWS = 8
BF16_ROWS = 16
VMEM_LIMIT = 60 * 1024 * 1024


def _params(sem):
    return pltpu.CompilerParams(dimension_semantics=sem, vmem_limit_bytes=VMEM_LIMIT)


def _const_spec(shape):
    nd = len(shape)
    return pl.BlockSpec(shape, lambda *_: (0,) * nd, pipeline_mode=pl.Buffered(1))


CAST_BLOCKS = 16


def _cast_jobs(items, n_steps, step_index):
    inputs, in_specs, out_specs, out_shapes = [], [], [], []
    if items:
        assert n_steps % CAST_BLOCKS == 0, n_steps
    per_block = n_steps // CAST_BLOCKS
    block_of = lambda *g: step_index(*g) // per_block
    for arr, layer in items:
        r, c = arr.shape[-2:]
        rows = r // CAST_BLOCKS
        assert rows * CAST_BLOCKS == r and rows % BF16_ROWS == 0, arr.shape
        if layer is None:
            in_specs.append(pl.BlockSpec((rows, c), lambda *g: (block_of(*g), 0)))
        else:
            in_specs.append(pl.BlockSpec((None, rows, c), lambda *g, layer=layer: (layer, block_of(*g), 0)))
        out_specs.append(pl.BlockSpec((rows, c), lambda *g: (block_of(*g), 0)))
        out_shapes.append(jax.ShapeDtypeStruct((r, c), BF16))
        inputs.append(arr)
    return inputs, in_specs, out_specs, out_shapes


def _run_cast_jobs(cast_in, cast_out):
    for src, dst in zip(cast_in, cast_out):
        dst[...] = src[...].astype(BF16)


def _rms_scale(x):
    return lax.rsqrt(jnp.mean(x * x, axis=-1, keepdims=True) + EPS)


def _silu_of_half(hx):
    return hx + hx * jnp.tanh(hx)


def _causal_conv(pre, hist_ref, w_ref, b_ref, c0, width, taps):
    tm = pre.shape[0]
    cols = slice(c0, c0 + width)
    ext = jnp.concatenate([hist_ref[:, cols], pre], axis=0)
    hist_ref[:, cols] = pre[tm - CARRY_ROWS:tm, :]
    w = [w_ref[k:k + 1, cols] for k in range(taps)]
    back1 = pltpu.roll(ext, 1, axis=0)
    if taps == 4:
        near = w[3] * ext + w[2] * back1
        far = w[1] * ext + w[0] * back1
        out = near + pltpu.roll(far, 2, axis=0)
    else:
        assert taps == 3
        far = w[1] * ext + w[0] * back1
        out = w[2] * ext + pltpu.roll(far, 1, axis=0)
    return out[CARRY_ROWS:, :] + b_ref[:, cols]


def _mamba_in_kernel(*refs, chunk, n_cast):
    x_ref, g_ref, w_ref, wdt_ref, cw_ref, cb_ref, dtb_ref = refs[:7]
    cast_in = refs[7:7 + n_cast]
    z_ref, xbc_ref, dt_ref = refs[7 + n_cast:10 + n_cast]
    cast_out = refs[10 + n_cast:10 + 2 * n_cast]
    (hist_ref,) = refs[10 + 2 * n_cast:]

    @pl.when(pl.program_id(1) == 0)
    def _():
        hist_ref[...] = jnp.zeros_like(hist_ref)

    x = x_ref[...]
    h = (x * _rms_scale(x) * g_ref[...]).astype(BF16)
    z_chunks = list(range(0, SSM_INNER, chunk))
    for i, c in enumerate(range(0, SSM_XBC, chunk)):
        pre = jnp.dot(h, w_ref[:, SSM_INNER + c:SSM_INNER + c + chunk], preferred_element_type=F32)
        conv = _causal_conv(pre, hist_ref, cw_ref, cb_ref, c, chunk, SSM_CONV)
        xbc_ref[:, c:c + chunk] = _silu_of_half(conv.astype(BF16))
        if i % 2 == 1:
            zc = z_chunks[i // 2]
            z_ref[:, zc:zc + chunk] = jnp.dot(h, w_ref[:, zc:zc + chunk], preferred_element_type=F32).astype(BF16)
    dtp = jnp.dot(h, wdt_ref[...], preferred_element_type=F32) + dtb_ref[...]
    dt_ref[...] = jnp.maximum(dtp, 0.0) + jnp.log1p(jnp.exp(-jnp.abs(dtp)))
    _run_cast_jobs(cast_in, cast_out)


def _mamba_in(x2, g, w, wdt, cw, cb, dtb, *, batch, seq, tm=1024, chunk=256, casts=()):
    nst = seq // tm
    row = lambda b, i: (b * nst + i, 0)
    t = batch * seq
    c_in, c_in_specs, c_out_specs, c_out_shapes = _cast_jobs(casts, batch * nst, lambda b, i: b * nst + i)
    return pl.pallas_call(
        functools.partial(_mamba_in_kernel, chunk=chunk, n_cast=len(c_in)),
        grid=(batch, nst),
        in_specs=[
            pl.BlockSpec((tm, D_MODEL), row),
            _const_spec((1, D_MODEL)),
            _const_spec((D_MODEL, SSM_PROJ)),
            _const_spec((D_MODEL, LANES)),
            _const_spec((SSM_CONV, SSM_XBC)),
            _const_spec((1, SSM_XBC)),
            _const_spec((1, LANES)),
        ] + c_in_specs,
        out_specs=[
            pl.BlockSpec((tm, SSM_INNER), row),
            pl.BlockSpec((tm, SSM_XBC), row),
            pl.BlockSpec((tm, LANES), row),
        ] + c_out_specs,
        out_shape=[
            jax.ShapeDtypeStruct((t, SSM_INNER), BF16),
            jax.ShapeDtypeStruct((t, SSM_XBC), BF16),
            jax.ShapeDtypeStruct((t, LANES), F32),
        ] + c_out_shapes,
        scratch_shapes=[pltpu.VMEM((CARRY_ROWS, SSM_XBC), F32)],
        compiler_params=_params(("arbitrary", "arbitrary")),
        name="mamba_in",
    )(x2, g, w, wdt, cw, cb, dtb, *c_in)


def _split3(a):
    hi = a.astype(BF16)
    r1 = a - hi.astype(F32)
    mid = r1.astype(BF16)
    lo = (r1 - mid.astype(F32)).astype(BF16)
    return hi, mid, lo


def _pack3(v, valid):
    hi, mid, lo = _split3(jnp.where(valid, v, 0.0))
    packed = (hi.astype(F32) + pltpu.roll(mid.astype(F32), SSM_HEADS, axis=1)
              + pltpu.roll(lo.astype(F32), 2 * SSM_HEADS, axis=1))
    return packed.astype(BF16)


def _ssd_kernel(*refs, ts, n_cast):
    xbc_ref, z_ref, dt_ref, x_ref, alog_ref, dx_ref, wo_ref = refs[:7]
    cast_in = refs[7:7 + n_cast]
    o_ref = refs[7 + n_cast]
    cast_out = refs[8 + n_cast:8 + 2 * n_cast]
    state_ref, yn_ref, sel_ref, sel64_ref = refs[8 + 2 * n_cast:]
    L = SSD_BLOCK
    GW = SSM_GROUP_WIDTH
    HPG = SSM_HEADS // SSM_GROUPS
    HD = SSM_HEAD_DIM
    b_off = SSM_INNER
    c_off = SSM_INNER + SSM_GROUPS * SSM_STATE

    @pl.when(pl.program_id(1) == 0)
    def _():
        state_ref[...] = jnp.zeros_like(state_ref)

    @pl.when((pl.program_id(0) == 0) & (pl.program_id(1) == 0))
    def _():
        for g in range(SSM_GROUPS):
            for ref, width in ((sel_ref, LANES), (sel64_ref, HD)):
                r = lax.broadcasted_iota(jnp.int32, (LANES, HPG * width), 0)
                e = lax.broadcasted_iota(jnp.int32, (LANES, HPG * width), 1) // width
                hit = (r < 3 * SSM_HEADS) & ((r % SSM_HEADS) == g * HPG + e)
                ref[g] = jnp.where(hit, 1.0, 0.0).astype(BF16)

    a_scale = -jnp.exp(alog_ref[...]) * LOG2E
    row = lax.broadcasted_iota(jnp.int32, (L, L), 0)
    col = lax.broadcasted_iota(jnp.int32, (L, L), 1)
    causal = row >= col
    tri = jnp.where(causal, 1.0, 0.0).astype(BF16)
    lane = lax.broadcasted_iota(jnp.int32, (L, LANES), 1)
    valid = lane < SSM_HEADS
    low = lane < HD
    lane_grp = lax.broadcasted_iota(jnp.int32, (L, GW), 1) // HD

    def block(c):
        r0 = pl.multiple_of(c * L, L)
        rows = pl.ds(r0, L)
        dt = dt_ref[rows, :]
        a = dt * a_scale
        acum = None
        for term in _split3(a):
            part = jnp.dot(tri, term, preferred_element_type=F32)
            acum = part if acum is None else acum + part
        acum_t = jnp.transpose(acum)
        acum_p = _pack3(acum, valid)
        dt_p = _pack3(dt, valid)

        batches = [range(g0, g0 + SSD_GROUP_BATCH) for g0 in range(0, SSM_GROUPS, SSD_GROUP_BATCH)]
        stages = (scan_load, scan_mid, scan_back)
        ctx = {}
        for t in range(len(batches) + max(SSD_STAGE_LAGS)):
            for stage, lag in zip(stages, SSD_STAGE_LAGS):
                j = t - lag
                if 0 <= j < len(batches):
                    ctx[j] = stage(rows, acum_t, acum_p, dt_p, batches[j], ctx.get(j))

    def scan_load(rows, acum_t, acum_p, dt_p, G, _):
        xs = {g: xbc_ref[rows, g * GW:(g + 1) * GW].astype(F32) for g in G}
        cg = {g: xbc_ref[rows, c_off + g * SSM_STATE:c_off + (g + 1) * SSM_STATE] for g in G}
        bgt = {g: jnp.transpose(xbc_ref[rows, b_off + g * SSM_STATE:b_off + (g + 1) * SSM_STATE]
                                .astype(F32)).astype(BF16) for g in G}
        acol = {g: jnp.dot(acum_p, sel_ref[g], preferred_element_type=F32) for g in G}
        dt_x = {g: jnp.dot(dt_p, sel64_ref[g], preferred_element_type=F32) for g in G}
        cb = {g: jnp.dot(cg[g], bgt[g], preferred_element_type=F32) for g in G}
        st = {g: state_ref[g] for g in G}
        y_in = {g: jnp.dot(cg[g], st[g].astype(BF16), preferred_element_type=F32) for g in G}
        return xs, bgt, st, y_in, acol, dt_x, cb

    def scan_mid(rows, acum_t, acum_p, dt_p, G, ctx):
        xs, bgt, st, y_in, acol, dt_x, cb = ctx
        xd, wcat, xcat, acum_x = {}, {}, {}, {}
        for g in G:
            ab = [acol[g][:, e * LANES:(e + 1) * LANES] for e in range(HPG)]
            acum_x[g] = jnp.concatenate([jnp.where(low, ab[0], ab[1]), jnp.where(low, ab[2], ab[3])], axis=1)
            xd[g] = xs[g] * dt_x[g]
            xd_b = xd[g].astype(BF16)
            ws, xparts = [], []
            for e in range(HPG):
                xparts.append(jnp.where(lane_grp == e, xd_b, jnp.zeros_like(xd_b)))
                seg = ab[e] - acum_t[g * HPG + e:g * HPG + e + 1, :]
                ws.append((cb[g] * jnp.exp2(jnp.where(causal, seg, -jnp.inf))).astype(BF16))
            wcat[g] = jnp.concatenate(ws, axis=1)
            xcat[g] = jnp.concatenate(xparts, axis=0)
        y = {g: jnp.dot(wcat[g], xcat[g], preferred_element_type=F32) for g in G}
        return xs, bgt, st, y_in, xd, acum_x, y

    def scan_back(rows, acum_t, acum_p, dt_p, G, ctx):
        xs, bgt, st, y_in, xd, acum_x, y = ctx
        to_end, alast = {}, {}
        for g in G:
            alast[g] = acum_x[g][L - 1:L, :]
            alast_x = alast[g]
            y[g] = y[g] + y_in[g] * jnp.exp2(acum_x[g])
            to_end[g] = (xd[g] * jnp.exp2(alast_x - acum_x[g])).astype(BF16)
        for g in G:
            alast_x = alast[g]
            state_ref[g] = st[g] * jnp.exp2(alast_x) + jnp.dot(bgt[g], to_end[g], preferred_element_type=F32)
        for g in G:
            yg = y[g] + xs[g] * dx_ref[:, g * GW:(g + 1) * GW]
            yg = yg * _silu_of_half(0.5 * z_ref[rows, g * GW:(g + 1) * GW].astype(F32))
            yn = yg * _rms_scale(yg)
            yn_ref[rows, g * GW:(g + 1) * GW] = yn.astype(BF16)

    def step(c, carry):
        block(c)
        return carry

    lax.fori_loop(0, ts // L, step, 0, unroll=True)
    o_ref[...] = x_ref[...] + jnp.dot(yn_ref[...], wo_ref[...], preferred_element_type=F32)
    _run_cast_jobs(cast_in, cast_out)


def _ssd(xbc, z, dt, x2, alog, dx, wo, *, batch, seq, ts=512, casts=()):
    nst = seq // ts
    row = lambda b, i: (b * nst + i, 0)
    t = batch * seq
    c_in, c_in_specs, c_out_specs, c_out_shapes = _cast_jobs(casts, batch * nst, lambda b, i: b * nst + i)
    return pl.pallas_call(
        functools.partial(_ssd_kernel, ts=ts, n_cast=len(c_in)),
        grid=(batch, nst),
        in_specs=[
            pl.BlockSpec((ts, SSM_XBC), row),
            pl.BlockSpec((ts, SSM_INNER), row),
            pl.BlockSpec((ts, LANES), row),
            pl.BlockSpec((ts, D_MODEL), row),
            _const_spec((1, LANES)),
            _const_spec((1, SSM_INNER)),
            _const_spec((SSM_INNER, D_MODEL)),
        ] + c_in_specs,
        out_specs=[pl.BlockSpec((ts, D_MODEL), row)] + c_out_specs,
        out_shape=[jax.ShapeDtypeStruct((t, D_MODEL), F32)] + c_out_shapes,
        scratch_shapes=[
            pltpu.VMEM((SSM_GROUPS, SSM_STATE, SSM_GROUP_WIDTH), F32),
            pltpu.VMEM((ts, SSM_INNER), BF16),
            pltpu.VMEM((SSM_GROUPS, LANES, (SSM_HEADS // SSM_GROUPS) * LANES), BF16),
            pltpu.VMEM((SSM_GROUPS, LANES, SSM_GROUP_WIDTH), BF16),
        ],
        compiler_params=_params(("arbitrary", "arbitrary")),
        name="ssd_out",
    )(xbc, z, dt, x2, alog, dx, wo, *c_in)


_FFN_CHUNKS = ((0, 512), (512, 512), (1024, 512), (1536, 512), (2048, 512), (2560, 256))


def _ffn_kernel(*refs, n_cast):
    x_ref, g_ref, win_ref, cw_ref, cb_ref, wd_ref = refs[:6]
    cast_in = refs[6:6 + n_cast]
    o_ref = refs[6 + n_cast]
    cast_out = refs[7 + n_cast:7 + 2 * n_cast]
    hist_ref, hid_ref = refs[7 + 2 * n_cast:]

    @pl.when(pl.program_id(1) == 0)
    def _():
        hist_ref[...] = jnp.zeros_like(hist_ref)

    x = x_ref[...]
    h = (x * _rms_scale(x) * g_ref[...]).astype(BF16)
    for c, width in _FFN_CHUNKS:
        gate = jnp.dot(h, win_ref[:, c:c + width], preferred_element_type=F32)
        val = jnp.dot(h, win_ref[:, FFN_DIM + c:FFN_DIM + c + width], preferred_element_type=F32)
        conv = _causal_conv(gate, hist_ref, cw_ref, cb_ref, c, width, FFN_CONV)
        hid_ref[:, c:c + width] = (_silu_of_half(conv) * val).astype(BF16)
    o_ref[...] = x + jnp.dot(hid_ref[...], wd_ref[...], preferred_element_type=F32)
    _run_cast_jobs(cast_in, cast_out)


def _ffn(x2, g, win, cw, cb, wd, *, batch, seq, tm=1024, casts=()):
    nst = seq // tm
    row = lambda b, i: (b * nst + i, 0)
    t = batch * seq
    c_in, c_in_specs, c_out_specs, c_out_shapes = _cast_jobs(casts, batch * nst, lambda b, i: b * nst + i)
    return pl.pallas_call(
        functools.partial(_ffn_kernel, n_cast=len(c_in)),
        grid=(batch, nst),
        in_specs=[
            pl.BlockSpec((tm, D_MODEL), row),
            _const_spec((1, D_MODEL)),
            _const_spec((D_MODEL, 2 * FFN_DIM)),
            _const_spec((FFN_CONV, FFN_DIM)),
            _const_spec((1, FFN_DIM)),
            _const_spec((FFN_DIM, D_MODEL)),
        ] + c_in_specs,
        out_specs=[pl.BlockSpec((tm, D_MODEL), row)] + c_out_specs,
        out_shape=[jax.ShapeDtypeStruct((t, D_MODEL), F32)] + c_out_shapes,
        scratch_shapes=[
            pltpu.VMEM((CARRY_ROWS, FFN_DIM), F32),
            pltpu.VMEM((tm, FFN_DIM), BF16),
        ],
        compiler_params=_params(("arbitrary", "arbitrary")),
        name="conv_ffn",
    )(x2, g, win, cw, cb, wd, *c_in)


def _rope_kernel(pos_ref, invf_ref, cos_ref, sin_ref, *, tp):
    per_row = LANES // (ATT_HEAD_DIM // 2)
    ang = pos_ref[...].astype(F32) * invf_ref[...]
    quarter = lax.broadcasted_iota(jnp.int32, (tp, LANES), 1) // (ATT_HEAD_DIM // 2)
    for arr, out_ref in ((jnp.cos(ang), cos_ref), (jnp.sin(ang), sin_ref)):
        rolled = [arr] + [pltpu.roll(arr, m * (ATT_HEAD_DIM // 2), axis=1) for m in range(1, per_row)]
        for j in range(per_row):
            t = rolled[(0 - j) % per_row]
            for k in range(1, per_row):
                t = jnp.where(quarter == k, rolled[(k - j) % per_row], t)
            out_ref[pl.ds(j, tp, stride=per_row), :] = t


def _rope_tables(pos_packed, invf, *, tp=512):
    rows = pos_packed.shape[0]
    tp = min(tp, rows)
    per_row = LANES // (ATT_HEAD_DIM // 2)
    return pl.pallas_call(
        functools.partial(_rope_kernel, tp=tp),
        grid=(rows // tp,),
        in_specs=[pl.BlockSpec((tp, LANES), lambda i: (i, 0)), _const_spec((1, LANES))],
        out_specs=[pl.BlockSpec((per_row * tp, LANES), lambda i: (i, 0))] * 2,
        out_shape=[jax.ShapeDtypeStruct((per_row * rows, LANES), F32)] * 2,
        compiler_params=_params(("arbitrary",)),
        name="rope_tables",
    )(pos_packed, invf)


def _head_norm_rope(y, gain_ref, cos, sin_signed, first_half, ones_bd):
    outs = []
    w = y.shape[1]
    for c in range(0, w, 2 * LANES):
        blk = y[:, c:c + 2 * LANES]
        ss = jnp.dot((blk * blk).astype(BF16), ones_bd, preferred_element_type=F32)
        yn = blk * lax.rsqrt(ss * (1.0 / ATT_HEAD_DIM) + EPS) * gain_ref[:, c:c + 2 * LANES]
        for j in range(0, 2 * LANES, LANES):
            v = yn[:, j:j + LANES]
            lo = pltpu.roll(v, ATT_HEAD_DIM // 2, axis=1)
            hi = pltpu.roll(v, LANES - ATT_HEAD_DIM // 2, axis=1)
            rot = jnp.where(first_half, hi, lo)
            outs.append(v * cos + rot * sin_signed)
    return outs


def _dup_heads(blocks, low):
    outs = []
    for v in blocks:
        swapped = pltpu.roll(v, ATT_HEAD_DIM, axis=1)
        outs.append(jnp.where(low, v, swapped).astype(BF16))
        outs.append(jnp.where(low, swapped, v).astype(BF16))
    return jnp.concatenate(outs, axis=1)


def _qkv_kernel(x_ref, cos_ref, sin_ref, gkv_ref, gq_ref, wkv_ref, bkv_ref, wq_ref, bq_ref,
                kn_ref, qn_ref, q_ref, k_ref, v_ref, *, tm):
    x = x_ref[...]
    xn = x * _rms_scale(x)
    hkv = (xn * gkv_ref[...]).astype(BF16)
    hq = (xn * gq_ref[...]).astype(BF16)
    kw = ATT_KV_HEADS * ATT_HEAD_DIM

    lane = lax.broadcasted_iota(jnp.int32, (tm, LANES), 1)
    first_half = (lane % ATT_HEAD_DIM) < (ATT_HEAD_DIM // 2)
    low = lane < ATT_HEAD_DIM
    cos = cos_ref[...]
    sin_signed = jnp.where(first_half, -sin_ref[...], sin_ref[...])
    r = lax.broadcasted_iota(jnp.int32, (2 * LANES, 2 * LANES), 0) // ATT_HEAD_DIM
    c = lax.broadcasted_iota(jnp.int32, (2 * LANES, 2 * LANES), 1) // ATT_HEAD_DIM
    ones_bd = jnp.where(r == c, 1.0, 0.0).astype(BF16)

    kv = jnp.dot(hkv, wkv_ref[...], preferred_element_type=F32) + bkv_ref[...]
    k_ref[...] = _dup_heads(_head_norm_rope(kv[:, :kw], kn_ref, cos, sin_signed, first_half, ones_bd), low)
    v_ref[...] = _dup_heads([kv[:, kw + j:kw + j + LANES] for j in range(0, kw, LANES)], low)
    q = jnp.dot(hq, wq_ref[...], preferred_element_type=F32) + bq_ref[...]
    qs = _head_norm_rope(q, qn_ref, cos, sin_signed, first_half, ones_bd)
    q_ref[...] = jnp.concatenate([v.astype(BF16) for v in qs], axis=1)


def _qkv(x2, cos, sin, gkv, gq, wkv, bkv, wq, bq, kn, qn, *, tm=1024):
    t = x2.shape[0]
    kw = ATT_KV_HEADS * ATT_HEAD_DIM
    qw = ATT_Q_HEADS * ATT_HEAD_DIM
    row = lambda i: (i, 0)
    return pl.pallas_call(
        functools.partial(_qkv_kernel, tm=tm),
        grid=(t // tm,),
        in_specs=[
            pl.BlockSpec((tm, D_MODEL), row),
            pl.BlockSpec((tm, LANES), row),
            pl.BlockSpec((tm, LANES), row),
            _const_spec((1, D_MODEL)),
            _const_spec((1, D_MODEL)),
            _const_spec((D_MODEL, 2 * kw)),
            _const_spec((1, 2 * kw)),
            _const_spec((D_MODEL, qw)),
            _const_spec((1, qw)),
            _const_spec((1, kw)),
            _const_spec((1, qw)),
        ],
        out_specs=[
            pl.BlockSpec((tm, qw), row),
            pl.BlockSpec((tm, 2 * kw), row),
            pl.BlockSpec((tm, 2 * kw), row),
        ],
        out_shape=[
            jax.ShapeDtypeStruct((t, qw), BF16),
            jax.ShapeDtypeStruct((t, 2 * kw), BF16),
            jax.ShapeDtypeStruct((t, 2 * kw), BF16),
        ],
        compiler_params=_params(("arbitrary",)),
        name="qkv_rope",
    )(x2, cos, sin, gkv, gq, wkv, bkv, wq, bq, kn, qn)


def _attn_kernel(sink_ref, q_ref, k_ref, v_ref, x_ref, wo_ref, bo_ref, o_ref, att_ref, *, tq):
    i = pl.program_id(1)
    W = WINDOW
    G = ATT_GROUP
    low = lax.broadcasted_iota(jnp.int32, (W, LANES), 1) < ATT_HEAD_DIM
    qi = lax.broadcasted_iota(jnp.int32, (W, 2 * W), 0)
    ki = lax.broadcasted_iota(jnp.int32, (W, 2 * W), 1)
    ones = jnp.ones((2 * W, LANES), BF16)

    for jb in range(tq // W):
        q0 = i * tq + jb * W
        start = pl.multiple_of(jnp.maximum(q0 - W, 0), W)
        rel = qi + (q0 - start) - ki
        neg = jnp.where((rel >= 0) & (rel < W), 0.0, -jnp.inf)
        rows = slice(jb * W, (jb + 1) * W)
        for hk in range(ATT_KV_HEADS):
            kh = k_ref[pl.ds(start, 2 * W), hk * LANES:(hk + 1) * LANES]
            vh = v_ref[pl.ds(start, 2 * W), hk * LANES:(hk + 1) * LANES]
            parts = []
            for g in range(G):
                qb = q_ref[rows, (hk * G + g) // 2 * LANES:((hk * G + g) // 2 + 1) * LANES]
                keep = low if g % 2 == 0 else jnp.logical_not(low)
                parts.append(jnp.where(keep, qb, jnp.zeros_like(qb)))
            qst = jnp.concatenate(parts, axis=0)
            s = lax.dot_general(qst, kh, (((1,), (1,)), ((), ())), preferred_element_type=F32)
            ps, ms = [], []
            for g in range(G):
                sg = s[g * W:(g + 1) * W, :] + neg
                m = jnp.maximum(jnp.max(sg, axis=-1, keepdims=True), sink_ref[hk * G + g])
                ps.append(jnp.exp(sg - m).astype(BF16))
                ms.append(m)
            pv = jnp.dot(jnp.concatenate(ps, axis=0), jnp.concatenate([vh, ones], axis=1),
                         preferred_element_type=F32)
            outs = []
            for g in range(G):
                blk = pv[g * W:(g + 1) * W, :]
                denom = blk[:, LANES:] + jnp.exp(sink_ref[hk * G + g] - ms[g])
                outs.append(blk[:, :LANES] / denom)
            for half in range(G // 2):
                col = (hk * G) // 2 + half
                att_ref[rows, col * LANES:(col + 1) * LANES] = jnp.where(
                    low, outs[2 * half], outs[2 * half + 1]).astype(BF16)
    o_ref[...] = (x_ref[...] + bo_ref[...]
                  + jnp.dot(att_ref[...], wo_ref[...], preferred_element_type=F32))


def _attn(sinks, q, k, v, x2, wo, bo, *, batch, seq, tq=512):
    nst = seq // tq
    qw = ATT_Q_HEADS * ATT_HEAD_DIM
    kw = ATT_KV_HEADS * ATT_HEAD_DIM
    row = lambda b, i: (b * nst + i, 0)
    per_batch = lambda b, i: (b, 0)
    t = batch * seq
    return pl.pallas_call(
        functools.partial(_attn_kernel, tq=tq),
        grid=(batch, nst),
        in_specs=[
            pl.BlockSpec(memory_space=pltpu.SMEM),
            pl.BlockSpec((tq, qw), row),
            pl.BlockSpec((seq, 2 * kw), per_batch),
            pl.BlockSpec((seq, 2 * kw), per_batch),
            pl.BlockSpec((tq, D_MODEL), row),
            _const_spec((qw, D_MODEL)),
            _const_spec((1, D_MODEL)),
        ],
        out_specs=pl.BlockSpec((tq, D_MODEL), row),
        out_shape=jax.ShapeDtypeStruct((t, D_MODEL), F32),
        scratch_shapes=[pltpu.VMEM((tq, qw), BF16)],
        compiler_params=_params(("arbitrary", "arbitrary")),
        name="swa_attn",
    )(sinks, q, k, v, x2, wo, bo)


def _row(v, width=None):
    v = v.reshape(1, -1).astype(F32)
    if width is not None and v.shape[1] < width:
        v = jnp.pad(v, ((0, 0), (0, width - v.shape[1])))
    return v


def kernel(x, positions, a_norm, a_in_proj, a_conv_w, a_conv_b, a_dt_bias, a_A_log, a_D, a_gnorm,
           a_out_proj, kv_norm, w_kv, b_kv, k_norm, b_norm, w_q, b_q, q_norm, sinks, w_o, b_o,
           f_norm, f_w_in, f_conv_w, f_conv_b, f_w_down):
    batch, seq, _ = x.shape
    t = batch * seq
    x2 = x.reshape(t, D_MODEL)

    w_in = a_in_proj.reshape(D_MODEL, SSM_PROJ)
    w_dt = jnp.pad(w_in[:, SSM_INNER + SSM_XBC:], ((0, 0), (0, LANES - SSM_HEADS))).astype(BF16)
    w_main = w_in.astype(BF16)
    z, xbc, dt = _mamba_in(
        x2, _row(a_norm[0]), w_main, w_dt, 0.5 * a_conv_w[0].astype(F32), 0.5 * _row(a_conv_b[0]),
        _row(a_dt_bias[0], LANES), batch=batch, seq=seq)
    dx = jnp.repeat(a_D[0].astype(F32), SSM_HEAD_DIM).reshape(1, SSM_INNER)
    wo_ssm = (a_gnorm[0].astype(F32)[:, None] * a_out_proj.reshape(SSM_INNER, D_MODEL)).astype(BF16)
    later = [(f_w_in, 0), (f_w_down, 0), (f_w_in, 1), (f_w_down, 1), (w_kv, None),
             (w_q.reshape(D_MODEL, -1), None), (w_o.reshape(-1, D_MODEL), None)]
    x2, win0, wd0, win1, wd1, wkv_b, wq_b, wo_b = _ssd(
        xbc, z, dt, x2, _row(a_A_log[0], LANES), dx, wo_ssm, batch=batch, seq=seq, casts=later)
    f_cw = 0.5 * f_conv_w.astype(F32)
    f_cb = 0.5 * f_conv_b.astype(F32)
    (x2,) = _ffn(x2, _row(f_norm[0]), win0, f_cw[0], _row(f_cb[0]), wd0, batch=batch, seq=seq)

    half = ATT_HEAD_DIM // 2
    inv_freq = ROPE_THETA ** (-jnp.arange(half, dtype=F32) / half)
    invf = jnp.tile(inv_freq, LANES // half).reshape(1, LANES)
    per_row = LANES // half
    pos_packed = jnp.repeat(positions.reshape(t // per_row, per_row), half, axis=1)
    cos, sin = _rope_tables(pos_packed, invf)
    q_gain = jnp.tile(q_norm[0].astype(F32), ATT_Q_HEADS) * (ATT_HEAD_DIM ** -0.5)
    q, k, v = _qkv(x2, cos, sin, _row(kv_norm), _row(b_norm[0]), wkv_b, _row(b_kv), wq_b, _row(b_q[0]),
                   _row(jnp.tile(k_norm, ATT_KV_HEADS)), _row(q_gain))
    x2 = _attn(sinks[0].astype(F32), q, k, v, x2, wo_b, _row(b_o[0]), batch=batch, seq=seq)
    (x2,) = _ffn(x2, _row(f_norm[1]), win1, f_cw[1], _row(f_cb[1]), wd1, batch=batch, seq=seq)
    return x2.reshape(batch, seq, D_MODEL)
```
